```python
import math, functools
import jax, jax.numpy as jnp
from jax import lax
import numpy as np

D_MODEL = 1024
BATCH = 4
SEQ = 4096
DEPTH = 1
DEC_BATCH = 32
DEC_SEQ = 8
PAST_LEN = 16384
PAGE_SIZE = 128

N_HEADS_A = 8
HEAD_DIM_A = 64
ROT_DIM = HEAD_DIM_A // 4
ROPE_THETA = 500000.0
N_HEADS_IDX = 4
IDX_DIM = 64
IDX_W_SCALE = (N_HEADS_IDX * IDX_DIM) ** -0.5
TOPK_MAX = 256
Q_BLOCK = 128
N_HEADS_B = 4
KEY_DIM_B = 128
VAL_DIM_B = 128
CHUNK_B = 16
WIDTH_A = N_HEADS_A * HEAD_DIM_A
WIDTH_B = N_HEADS_B * VAL_DIM_B
KEYW_B = N_HEADS_B * KEY_DIM_B
D_FF = ((8 * D_MODEL // 3 + 127) // 128) * 128
EPS = 1e-6
IN_SIZES = (WIDTH_A, WIDTH_A, WIDTH_A, N_HEADS_IDX * IDX_DIM, IDX_DIM, N_HEADS_IDX, KEYW_B, KEYW_B, WIDTH_B, WIDTH_B, D_MODEL, D_MODEL)
IN_OFFSETS = tuple(int(o) for o in np.cumsum(IN_SIZES)[:-1])
N_IN = int(sum(IN_SIZES))

kernel_name = 'dsa_hgrn2_gated_macaron_step'


def rms_norm(x, g):
    xf = x.astype(jnp.float32)
    y = xf * lax.rsqrt(jnp.mean(xf * xf, axis=-1, keepdims=True) + EPS)
    return (y * g.astype(jnp.float32)).astype(x.dtype)


def swiglu(h, w_gate, w_up, w_down):
    return (jax.nn.silu(h @ w_gate) * (h @ w_up)) @ w_down


def rotary(x, pos):
    inv = ROPE_THETA ** (-jnp.arange(0, ROT_DIM, 2, dtype=jnp.float32) / ROT_DIM)
    ang = pos.astype(jnp.float32)[:, None] * inv[None, :]
    cos = jnp.cos(ang)[:, None, :]
    sin = jnp.sin(ang)[:, None, :]
    xr = x[..., :ROT_DIM].astype(jnp.float32)
    x1, x2 = xr[..., :ROT_DIM // 2], xr[..., ROT_DIM // 2:]
    rot = jnp.concatenate([x1 * cos - x2 * sin, x2 * cos + x1 * sin], axis=-1).astype(x.dtype)
    return jnp.concatenate([rot, x[..., ROT_DIM:]], axis=-1)


def gather_rows(src, idx):
    return jax.vmap(lambda s, i: s[i])(src, idx)


def indexed_sparse_attention(q, iq, iw, qpos, ik_all, kv_gather):
    L = ik_all.shape[1]
    n_sel = min(TOPK_MAX, L // 4)
    dots = jnp.einsum('bthd,bsd->bths', iq.astype(jnp.float32), ik_all.astype(jnp.float32))
    score = jnp.einsum('bths,bth->bts', jax.nn.relu(dots), iw.astype(jnp.float32) * IDX_W_SCALE)
    admissible = jnp.arange(L, dtype=jnp.int32)[None, :] <= qpos[:, None]
    score = jnp.where(admissible[None], score, -jnp.inf)
    _, idx = lax.top_k(score, n_sel)
    valid = idx <= qpos[None, :, None]
    k_sel, v_sel = kv_gather(idx)
    logits = jnp.einsum('bthd,btkhd->bthk', q.astype(jnp.float32), k_sel.astype(jnp.float32)) * (HEAD_DIM_A ** -0.5)
    logits = jnp.where(valid[:, :, None, :], logits, -jnp.inf)
    p = jax.nn.softmax(logits, axis=-1)
    return jnp.einsum('bthk,btkhd->bthd', p.astype(v_sel.dtype), v_sel)


def prompt_attention(q, k, v, iq, ik, iw, pos):
    B, T = q.shape[:2]
    nb = T // Q_BLOCK

    def blocks(a):
        return jnp.moveaxis(a.reshape((B, nb, Q_BLOCK) + a.shape[2:]), 1, 0)

    def kv_gather(idx):
        return gather_rows(k, idx), gather_rows(v, idx)

    def body(blk):
        qb, iqb, iwb, pb = blk
        return indexed_sparse_attention(qb, iqb, iwb, pb, ik, kv_gather)

    out = lax.map(body, (blocks(q), blocks(iq), blocks(iw), pos.reshape(nb, Q_BLOCK)))
    return jnp.moveaxis(out, 0, 1).reshape(B, T, N_HEADS_A, HEAD_DIM_A)


def sample_attention(layer, cache_k, cache_v, cache_idx_k, page_table, q, k, v, iq, ik, iw, pos):
    B, T = q.shape[:2]
    ik_past = cache_idx_k[layer, page_table].reshape(B, PAST_LEN, IDX_DIM)
    ik_all = jnp.concatenate([ik_past.astype(ik.dtype), ik], axis=1)

    def kv_gather(idx):
        is_past = idx < PAST_LEN
        pidx = jnp.minimum(idx, PAST_LEN - 1)
        phys = jax.vmap(lambda pt, i: pt[i])(page_table, pidx // PAGE_SIZE)
        off = pidx % PAGE_SIZE
        kp = cache_k[layer, phys, off].astype(k.dtype)
        vp = cache_v[layer, phys, off].astype(v.dtype)
        nidx = jnp.clip(idx - PAST_LEN, 0, T - 1)
        m = is_past[..., None, None]
        return jnp.where(m, kp, gather_rows(k, nidx)), jnp.where(m, vp, gather_rows(v, nidx))

    return indexed_sparse_attention(q, iq, iw, pos, ik_all, kv_gather)


def hgrn2_chunked(q, k, logf, v, s0):
    B, T, H, DK = q.shape
    DV = v.shape[-1]
    n_chunks = -(-T // CHUNK_B)
    pad = n_chunks * CHUNK_B - T

    def to_chunks(a):
        a = jnp.pad(a.astype(jnp.float32), ((0, 0), (0, pad), (0, 0), (0, 0)))
        return a.reshape(B, n_chunks, CHUNK_B, H, a.shape[-1]).transpose(1, 0, 3, 2, 4)

    causal = jnp.tril(jnp.ones((CHUNK_B, CHUNK_B), dtype=bool))[:, :, None]

    def step(S, chunk):
        qc, kc, gc, vc = chunk
        b = jnp.cumsum(gc, axis=2)
        diff = jnp.where(causal, b[:, :, :, None, :] - b[:, :, None, :, :], -jnp.inf)
        att = jnp.einsum('bhtk,bhsk,bhtsk->bhts', qc, kc, jnp.exp(diff))
        out = jnp.einsum('bhtk,bhkv->bhtv', qc * jnp.exp(b), S) + jnp.einsum('bhts,bhsv->bhtv', att, vc)
        b_last = b[:, :, -1:, :]
        S = jnp.exp(b_last[:, :, 0, :])[..., None] * S + jnp.einsum('bhsk,bhsv->bhkv', kc * jnp.exp(b_last - b), vc)
        return S, out

    S, out = lax.scan(step, s0.astype(jnp.float32), (to_chunks(q), to_chunks(k), to_chunks(logf), to_chunks(v)))
    out = out.transpose(1, 0, 3, 2, 4).reshape(B, n_chunks * CHUNK_B, H, DV)[:, :T]
    return out, S.astype(s0.dtype)


def mixer_block(h, pos, attend, s0, w_in, q_norm, k_norm, lb, o_norm, w_proj_attn, w_proj_hgrn, w_out):
    B, T, _ = h.shape
    (q, k, v, iq, ik, iw, hq, hf, hi, hg, ga, gb) = jnp.split(h @ w_in, IN_OFFSETS, axis=-1)
    q = rotary(rms_norm(q.reshape(B, T, N_HEADS_A, HEAD_DIM_A), q_norm), pos)
    k = rotary(rms_norm(k.reshape(B, T, N_HEADS_A, HEAD_DIM_A), k_norm), pos)
    v = v.reshape(B, T, N_HEADS_A, HEAD_DIM_A)
    iq = rotary(iq.reshape(B, T, N_HEADS_IDX, IDX_DIM), pos)
    ik = rotary(ik[:, :, None, :], pos)[:, :, 0, :]
    attn = attend(q, k, v, iq, ik, iw, pos).reshape(B, T, WIDTH_A)
    zf = hf.astype(jnp.float32)
    logf = jnp.log(lb + (1.0 - lb) * jax.nn.sigmoid(zf))
    kf = (1.0 - lb) * jax.nn.sigmoid(-zf)
    o, s_new = hgrn2_chunked(jax.nn.silu(hq).reshape(B, T, N_HEADS_B, KEY_DIM_B),
                             kf.reshape(B, T, N_HEADS_B, KEY_DIM_B),
                             logf.reshape(B, T, N_HEADS_B, KEY_DIM_B),
                             hi.reshape(B, T, N_HEADS_B, VAL_DIM_B), s0)
    o = rms_norm(o.astype(h.dtype), o_norm) * jax.nn.silu(hg.reshape(B, T, N_HEADS_B, VAL_DIM_B))
    merged = jax.nn.sigmoid(ga) * (attn @ w_proj_attn) + jax.nn.sigmoid(gb) * (o.reshape(B, T, WIDTH_B) @ w_proj_hgrn)
    return merged @ w_out, k, v, ik, s_new


def decoder_layer(x, pos, attend, s0, ffn1, mix, ffn2):
    x = x + 0.5 * swiglu(rms_norm(x, ffn1[0]), ffn1[1], ffn1[2], ffn1[3])
    y, k, v, ik, s_new = mixer_block(rms_norm(x, mix[0]), pos, attend, s0, *mix[1:])
    x = x + y
    x = x + 0.5 * swiglu(rms_norm(x, ffn2[0]), ffn2[1], ffn2[2], ffn2[3])
    return x, k, v, ik, s_new


def setup_inputs(seed: int = 0) -> dict:
    key = jax.random.key(seed)
    ks = jax.random.split(key, 26)
    n_pages = PAST_LEN // PAGE_SIZE
    n_pool = (DEC_BATCH * n_pages * 5) // 4

    def w(k, shape, fan_in):
        return jax.random.normal(k, shape, jnp.float32) * (fan_in ** -0.5)

    def gain(k, shape):
        return 1.0 + 0.02 * jax.random.normal(k, shape, jnp.float32)

    page_table = jax.random.permutation(ks[6], n_pool)[:DEC_BATCH * n_pages].reshape(DEC_BATCH, n_pages).astype(jnp.int32)
    return {
        'x_prompt': jax.random.normal(ks[0], (BATCH, SEQ, D_MODEL), jnp.float32),
        'x_sample': jax.random.normal(ks[1], (DEC_BATCH, DEC_SEQ, D_MODEL), jnp.float32),
        'cache_k': jax.random.normal(ks[2], (DEPTH, n_pool, PAGE_SIZE, N_HEADS_A, HEAD_DIM_A), jnp.float32),
        'cache_v': jax.random.normal(ks[3], (DEPTH, n_pool, PAGE_SIZE, N_HEADS_A, HEAD_DIM_A), jnp.float32),
        'cache_idx_k': jax.random.normal(ks[4], (DEPTH, n_pool, PAGE_SIZE, IDX_DIM), jnp.float32),
        'state_hgrn': 0.5 * jax.random.normal(ks[5], (DEPTH, DEC_BATCH, N_HEADS_B, KEY_DIM_B, VAL_DIM_B), jnp.float32),
        'page_table': page_table,
        'ffn1_norm': gain(ks[7], (DEPTH, D_MODEL)),
        'ffn1_w_gate': w(ks[8], (DEPTH, D_MODEL, D_FF), D_MODEL),
        'ffn1_w_up': w(ks[9], (DEPTH, D_MODEL, D_FF), D_MODEL),
        'ffn1_w_down': w(ks[10], (DEPTH, D_FF, D_MODEL), D_FF),
        'mix_norm': gain(ks[11], (DEPTH, D_MODEL)),
        'w_in': w(ks[12], (DEPTH, D_MODEL, N_IN), D_MODEL),
        'q_norm': gain(ks[13], (DEPTH, HEAD_DIM_A)),
        'k_norm': gain(ks[14], (DEPTH, HEAD_DIM_A)),
        'hgrn_lb': 0.5 * jax.random.normal(ks[15], (DEPTH + 1, KEYW_B), jnp.float32),
        'hgrn_o_norm': gain(ks[16], (DEPTH, VAL_DIM_B)),
        'w_proj_attn': w(ks[17], (DEPTH, WIDTH_A, D_MODEL), WIDTH_A),
        'w_proj_hgrn': w(ks[18], (DEPTH, WIDTH_B, D_MODEL), WIDTH_B),
        'w_out': w(ks[19], (DEPTH, D_MODEL, D_MODEL), D_MODEL),
        'ffn2_norm': gain(ks[20], (DEPTH, D_MODEL)),
        'ffn2_w_gate': w(ks[21], (DEPTH, D_MODEL, D_FF), D_MODEL),
        'ffn2_w_up': w(ks[22], (DEPTH, D_MODEL, D_FF), D_MODEL),
        'ffn2_w_down': w(ks[23], (DEPTH, D_FF, D_MODEL), D_FF),
    }


def reference(x_prompt, x_sample, cache_k, cache_v, cache_idx_k, state_hgrn, page_table,
              ffn1_norm, ffn1_w_gate, ffn1_w_up, ffn1_w_down, mix_norm, w_in, q_norm, k_norm,
              hgrn_lb, hgrn_o_norm, w_proj_attn, w_proj_hgrn, w_out,
              ffn2_norm, ffn2_w_gate, ffn2_w_up, ffn2_w_down):
    pos_p = jnp.arange(SEQ, dtype=jnp.int32)
    pos_s = PAST_LEN + jnp.arange(DEC_SEQ, dtype=jnp.int32)
    lb_all = jnp.cumsum(jax.nn.softmax(hgrn_lb.astype(jnp.float32), axis=0), axis=0)
    xp, xs = x_prompt, x_sample
    kp_l, vp_l, ikp_l, sp_l, ks_l, vs_l, iks_l, ss_l = [], [], [], [], [], [], [], []
    for l in range(DEPTH):
        ffn1 = (ffn1_norm[l], ffn1_w_gate[l], ffn1_w_up[l], ffn1_w_down[l])
        ffn2 = (ffn2_norm[l], ffn2_w_gate[l], ffn2_w_up[l], ffn2_w_down[l])
        mix = (mix_norm[l], w_in[l], q_norm[l], k_norm[l], lb_all[l], hgrn_o_norm[l],
               w_proj_attn[l], w_proj_hgrn[l], w_out[l])
        s0_p = jnp.zeros((xp.shape[0], N_HEADS_B, KEY_DIM_B, VAL_DIM_B), xp.dtype)
        xp, kp, vp, ikp, sp = decoder_layer(xp, pos_p, prompt_attention, s0_p, ffn1, mix, ffn2)
        attend_s = functools.partial(sample_attention, l, cache_k, cache_v, cache_idx_k, page_table)
        xs, ks, vs, iks, ss = decoder_layer(xs, pos_s, attend_s, state_hgrn[l], ffn1, mix, ffn2)
        kp_l.append(kp); vp_l.append(vp); ikp_l.append(ikp); sp_l.append(sp)
        ks_l.append(ks); vs_l.append(vs); iks_l.append(iks); ss_l.append(ss)
    return (xp, xs,
            jnp.stack(kp_l), jnp.stack(vp_l), jnp.stack(ikp_l), jnp.stack(sp_l),
            jnp.stack(ks_l), jnp.stack(vs_l), jnp.stack(iks_l), jnp.stack(ss_l))
```

```python
import functools

import numpy as np
import jax
import jax.numpy as jnp
from jax import lax
from jax.experimental import pallas as pl
from jax.experimental.pallas import tpu as pltpu

F32 = jnp.float32
BF16 = jnp.bfloat16
I32 = jnp.int32

D_MODEL = 1024
PAST_LEN = 16384
PAGE_SIZE = 128
N_HEADS_A = 8
HEAD_DIM_A = 64
ROT_DIM = HEAD_DIM_A // 4
ROPE_THETA = 500000.0
N_HEADS_IDX = 4
IDX_DIM = 64
IDX_W_SCALE = (N_HEADS_IDX * IDX_DIM) ** -0.5
TOPK_MAX = 256
N_HEADS_B = 4
KEY_DIM_B = 128
VAL_DIM_B = 128
CHUNK_B = 16
WIDTH_A = N_HEADS_A * HEAD_DIM_A
WIDTH_B = N_HEADS_B * VAL_DIM_B
KEYW_B = N_HEADS_B * KEY_DIM_B
EPS = 1e-6
IN_SIZES = (WIDTH_A, WIDTH_A, WIDTH_A, N_HEADS_IDX * IDX_DIM, IDX_DIM, N_HEADS_IDX,
            KEYW_B, KEYW_B, WIDTH_B, WIDTH_B, D_MODEL, D_MODEL)
IN_OFFSETS = tuple(int(o) for o in np.cumsum(IN_SIZES)[:-1])

LANES = 128
SUBLANES = 8
VMEM_LIMIT_BYTES = 56 * 1024 * 1024
INT_MIN = np.int32(-2 ** 31)
NEG_BIG = -1e30
FF_CHUNK = 256
IN_CHUNK = 512
NT_DIMS = (((1,), (1,)), ((), ()))


def _params(*sem):
    return pltpu.CompilerParams(dimension_semantics=sem, vmem_limit_bytes=VMEM_LIMIT_BYTES)


def _float_key(x):
    u = pltpu.bitcast(x + 0.0, I32)
    return u ^ ((u >> 31) & np.int32(0x7FFFFFFF))


def _ffn_kernel(x_ref, g_ref, wg_ref, wu_ref, wd_ref, o_ref, h_scr, acc_scr):
    j = pl.program_id(1)

    @pl.when(j == 0)
    def _():
        x = x_ref[...]
        ms = jnp.mean(x * x, axis=-1, keepdims=True)
        h_scr[...] = (x * lax.rsqrt(ms + EPS) * g_ref[...]).astype(BF16)
        acc_scr[...] = jnp.zeros_like(acc_scr)

    h = h_scr[...]
    g = jnp.dot(h, wg_ref[...], preferred_element_type=F32)
    u = jnp.dot(h, wu_ref[...], preferred_element_type=F32)
    a = (g * jax.nn.sigmoid(g) * u).astype(BF16)
    acc_scr[...] += jnp.dot(a, wd_ref[...], preferred_element_type=F32)

    @pl.when(j == pl.num_programs(1) - 1)
    def _():
        o_ref[...] = x_ref[...] + 0.5 * acc_scr[...]


def _ffn(x, gain, wg, wu, wd, tm):
    n, d = x.shape
    ff = wg.shape[1]
    return pl.pallas_call(
        _ffn_kernel,
        grid=(n // tm, ff // FF_CHUNK),
        in_specs=[
            pl.BlockSpec((tm, d), lambda i, j: (i, 0)),
            pl.BlockSpec((1, d), lambda i, j: (0, 0)),
            pl.BlockSpec((d, FF_CHUNK), lambda i, j: (0, j)),
            pl.BlockSpec((d, FF_CHUNK), lambda i, j: (0, j)),
            pl.BlockSpec((FF_CHUNK, d), lambda i, j: (j, 0)),
        ],
        out_specs=pl.BlockSpec((tm, d), lambda i, j: (i, 0)),
        out_shape=jax.ShapeDtypeStruct((n, d), F32),
        scratch_shapes=[pltpu.VMEM((tm, d), BF16), pltpu.VMEM((tm, d), F32)],
        compiler_params=_params("parallel", "arbitrary"),
        name="ffn",
    )(x, gain.reshape(1, d), wg, wu, wd)


_J_Q, _J_K, _J_V, _J_IDX, _J_HQ, _J_HF, _J_HI, _J_HG, _J_GA0, _J_GA1, _J_GB0, _J_GB1 = range(12)


def _rot(xc, c, sa, sb):
    return (xc * c + pltpu.roll(xc, LANES - ROT_DIM // 2, 1) * sa
            + pltpu.roll(xc, ROT_DIM // 2, 1) * sb)


def _inproj_kernel(x_ref, g_ref, w_ref, bd_ref, qg_ref, kg_ref, lb_ref, cos_ref, sa_ref, sb_ref,
                   qm_ref, k_ref, kb_ref, v_ref, vb_ref, iqm_ref, ikw_ref, ik2_ref,
                   hq_ref, kf_ref, lf_ref, hi_ref, hg_ref, ga_ref, gb_ref, h_scr):
    j = pl.program_id(1)
    tm = x_ref.shape[0]

    @pl.when(j == 0)
    def _():
        x = x_ref[...]
        ms = jnp.mean(x * x, axis=-1, keepdims=True)
        h_scr[...] = (x * lax.rsqrt(ms + EPS) * g_ref[...]).astype(BF16)

    y = jnp.dot(h_scr[...], w_ref[0], preferred_element_type=F32)
    lane = lax.broadcasted_iota(I32, (tm, LANES), 1)
    lo_half = lane < HEAD_DIM_A

    def head_norm(t, gain):
        sq = t * t
        hi = sq.astype(BF16)
        lo = (sq - hi.astype(F32)).astype(BF16)
        ms = (jnp.dot(hi, bd_ref[...], preferred_element_type=F32)
              + jnp.dot(lo, bd_ref[...], preferred_element_type=F32))
        return t * lax.rsqrt(ms + EPS) * gain

    @pl.when(j == _J_Q)
    def _():
        yn = head_norm(y, qg_ref[...])
        c, sa, sb = cos_ref[...], sa_ref[...], sb_ref[...]
        for p in range(WIDTH_A // LANES):
            r = _rot(yn[:, p * LANES:(p + 1) * LANES], c, sa, sb) * (HEAD_DIM_A ** -0.5)
            qm_ref[:, (2 * p) * LANES:(2 * p + 1) * LANES] = jnp.where(lo_half, r, 0.0).astype(BF16)
            qm_ref[:, (2 * p + 1) * LANES:(2 * p + 2) * LANES] = jnp.where(lo_half, 0.0, r).astype(BF16)

    @pl.when(j == _J_K)
    def _():
        yn = head_norm(y, kg_ref[...])
        c, sa, sb = cos_ref[...], sa_ref[...], sb_ref[...]
        for p in range(WIDTH_A // LANES):
            sl = slice(p * LANES, (p + 1) * LANES)
            r = _rot(yn[:, sl], c, sa, sb)
            k_ref[:, sl] = r
            kb_ref[:, sl] = r.astype(BF16)

    @pl.when(j == _J_V)
    def _():
        v_ref[...] = y
        vb_ref[...] = y.astype(BF16)

    @pl.when(j == _J_IDX)
    def _():
        c, sa, sb = cos_ref[...], sa_ref[...], sb_ref[...]
        for p in range(2):
            r = _rot(y[:, p * LANES:(p + 1) * LANES], c, sa, sb)
            iqm_ref[:, (2 * p) * LANES:(2 * p + 1) * LANES] = jnp.where(lo_half, r, 0.0).astype(BF16)
            iqm_ref[:, (2 * p + 1) * LANES:(2 * p + 2) * LANES] = jnp.where(lo_half, 0.0, r).astype(BF16)
        r = _rot(y[:, 2 * LANES:3 * LANES], jnp.where(lo_half, c, 1.0),
                 jnp.where(lo_half, sa, 0.0), jnp.where(lo_half, sb, 0.0))
        ikw_ref[...] = r
        ik2_ref[...] = jnp.where(lo_half, r, pltpu.roll(r, HEAD_DIM_A, 1)).astype(BF16)

    @pl.when(j == _J_HQ)
    def _():
        hq_ref[...] = y * jax.nn.sigmoid(y)

    @pl.when(j == _J_HF)
    def _():
        lb = lb_ref[...]
        lf_ref[...] = jnp.log(lb + (1.0 - lb) * jax.nn.sigmoid(y))
        kf_ref[...] = (1.0 - lb) * jax.nn.sigmoid(-y)

    @pl.when(j == _J_HI)
    def _():
        hi_ref[...] = y

    @pl.when(j == _J_HG)
    def _():
        hg_ref[...] = y * jax.nn.sigmoid(y)

    for jj, ref, half in ((_J_GA0, ga_ref, 0), (_J_GA1, ga_ref, 1), (_J_GB0, gb_ref, 0), (_J_GB1, gb_ref, 1)):
        @pl.when(j == jj)
        def _(ref=ref, half=half):
            ref[:, half * IN_CHUNK:(half + 1) * IN_CHUNK] = jax.nn.sigmoid(y)


def _inproj(x, gain, w12, bd, qg, kg, lb, cos_t, sa_t, sb_t, tm):
    n, d = x.shape
    nt = cos_t.shape[0] // tm
    row = lambda w: pl.BlockSpec((tm, w), lambda i, j: (i, 0))
    const = lambda r, w: pl.BlockSpec((r, w), lambda i, j: (0, 0))
    tab = pl.BlockSpec((tm, LANES), lambda i, j: (i % nt, 0))
    out_widths = [(2 * WIDTH_A, BF16), (WIDTH_A, F32), (WIDTH_A, BF16), (WIDTH_A, F32), (WIDTH_A, BF16),
                  (2 * N_HEADS_IDX * IDX_DIM, BF16), (LANES, F32), (LANES, BF16),
                  (KEYW_B, F32), (KEYW_B, F32), (KEYW_B, F32), (WIDTH_B, F32), (WIDTH_B, F32),
                  (D_MODEL, F32), (D_MODEL, F32)]
    return pl.pallas_call(
        _inproj_kernel,
        grid=(n // tm, w12.shape[0]),
        in_specs=[
            row(d), const(1, d),
            pl.BlockSpec((1, d, IN_CHUNK), lambda i, j: (j, 0, 0)),
            const(WIDTH_A, WIDTH_A), const(1, WIDTH_A), const(1, WIDTH_A), const(1, KEYW_B),
            tab, tab, tab,
        ],
        out_specs=[row(w) for w, _ in out_widths],
        out_shape=[jax.ShapeDtypeStruct((n, w), dt) for w, dt in out_widths],
        scratch_shapes=[pltpu.VMEM((tm, d), BF16)],
        compiler_params=_params("parallel", "arbitrary"),
        name="inproj",
    )(x, gain.reshape(1, d), w12, bd, qg, kg, lb, cos_t, sa_t, sb_t)


def _search_threshold(count_ge, rows, n_sel):
    c0 = count_ge(jnp.zeros((rows, 1), I32))
    cand = jnp.where(c0 >= n_sel, np.int32(0), INT_MIN)

    def bit_body(i, cand):
        t = cand + (jnp.int32(1) << (30 - i))
        return jnp.where(count_ge(t) >= n_sel, t, cand)

    return lax.fori_loop(0, 31, bit_body, cand)


def _pattn_kernel(iqm_ref, ikw_ref, ik2_ref, qm_ref, k_ref, v_ref, o_ref,
                  keys_scr, thr_scr, need_scr, flag_scr, m_scr, l_scr, acc_scr, *, n_sel, tq, rg):
    qi = pl.program_id(1)
    nkb = qi + 1
    lane = lax.broadcasted_iota(I32, (tq, LANES), 1)
    lo_half = lane < HEAD_DIM_A
    row = lax.broadcasted_iota(I32, (tq, tq), 0)
    col = lax.broadcasted_iota(I32, (tq, tq), 1)

    ikw = ikw_ref[0]
    wcol = [ikw[:, IDX_DIM + h:IDX_DIM + h + 1] * IDX_W_SCALE for h in range(N_HEADS_IDX)]

    def score_blk(kb, c):
        ik2 = ik2_ref[0, pl.ds(pl.multiple_of(kb * tq, tq), tq), :]
        sc = jnp.zeros((tq, tq), F32)
        for h in range(N_HEADS_IDX):
            d = lax.dot_general(iqm_ref[0, :, h * LANES:(h + 1) * LANES], ik2, NT_DIMS,
                                preferred_element_type=F32)
            sc = sc + jnp.maximum(d, 0.0) * wcol[h]
        off = jnp.where(kb < qi, jnp.int32(tq), jnp.int32(0))
        keys_scr[kb] = jnp.where(col <= row + off, _float_key(sc), INT_MIN)
        return c

    lax.fori_loop(0, nkb, score_blk, 0)

    for g in range(tq // rg):
        r0 = g * rg

        def count_ge(t, r0=r0):
            def body(kb, acc):
                kk = keys_scr[kb, r0:r0 + rg, :]
                for c in range(tq // LANES):
                    acc = acc + jnp.where(kk[:, c * LANES:(c + 1) * LANES] >= t, 1.0, 0.0)
                return acc
            acc = lax.fori_loop(0, nkb, body, jnp.zeros((rg, LANES), F32))
            return jnp.sum(acc, axis=1, keepdims=True)

        cand = _search_threshold(count_ge, rg, n_sel)
        cge = count_ge(cand)
        cgt = count_ge(cand + 1)
        thr_scr[r0:r0 + rg, :] = jnp.broadcast_to(cand, (rg, LANES))
        need_scr[r0:r0 + rg, :] = jnp.broadcast_to(n_sel - cgt, (rg, LANES))
        flag_scr[r0:r0 + rg, :] = jnp.broadcast_to(
            jnp.where(cge > n_sel, jnp.where(cand > INT_MIN, 1.0, 0.0), 0.0), (rg, LANES))

    @pl.when(jnp.max(flag_scr[...]) > 0.0)
    def _():
        cand = thr_scr[:, :1]
        need = need_scr[:, :1]
        upper = jnp.where(row < col, 1.0, 0.0).astype(BF16)

        def tie_blk(kb, seen):
            kk = keys_scr[kb]
            tf = jnp.where(kk == cand, 1.0, 0.0)
            rank = jnp.dot(tf.astype(BF16), upper, preferred_element_type=F32) + seen
            drop = jnp.where(rank >= need, tf, 0.0)
            keys_scr[kb] = jnp.where(drop > 0.0, INT_MIN, kk)
            return seen + jnp.sum(tf, axis=1, keepdims=True)

        lax.fori_loop(0, nkb, tie_blk, jnp.zeros((tq, 1), F32))

    thr = jnp.maximum(thr_scr[:, :1], INT_MIN + 1)
    m_scr[...] = jnp.full(m_scr.shape, NEG_BIG, F32)
    l_scr[...] = jnp.zeros(l_scr.shape, F32)
    acc_scr[...] = jnp.zeros(acc_scr.shape, F32)

    def attn_blk(kb, c):
        sel = keys_scr[kb] >= thr
        ks = pl.ds(pl.multiple_of(kb * tq, tq), tq)
        for h in range(N_HEADS_A):
            pr = slice((h // 2) * LANES, (h // 2 + 1) * LANES)
            s = lax.dot_general(qm_ref[0, :, h * LANES:(h + 1) * LANES], k_ref[0, ks, pr], NT_DIMS,
                                preferred_element_type=F32)
            s = jnp.where(sel, s, NEG_BIG)
            m_old = m_scr[h, :, :1]
            m_new = jnp.maximum(m_old, jnp.max(s, axis=1, keepdims=True))
            alpha = jnp.exp(m_old - m_new)
            p = jnp.exp(s - m_new)
            l_scr[h] = alpha * l_scr[h] + jnp.sum(p, axis=1, keepdims=True)
            acc_scr[h] = alpha * acc_scr[h] + jnp.dot(p.astype(BF16), v_ref[0, ks, pr],
                                                     preferred_element_type=F32)
            m_scr[h] = jnp.broadcast_to(m_new, (tq, LANES))
        return c

    lax.fori_loop(0, nkb, attn_blk, 0)
    for p in range(N_HEADS_A // 2):
        a0 = acc_scr[2 * p] / l_scr[2 * p]
        a1 = acc_scr[2 * p + 1] / l_scr[2 * p + 1]
        o_ref[0, :, p * LANES:(p + 1) * LANES] = jnp.where(lo_half, a0, a1)


def _prompt_attention(iqm, ikw, ik2, qm, kb, vb, n_sel, tq):
    b, t, _ = qm.shape
    nq = t // tq
    qblk = lambda w: pl.BlockSpec((1, tq, w), lambda bi, qi: (bi, qi, 0))
    full = lambda w: pl.BlockSpec((1, t, w), lambda bi, qi: (bi, 0, 0))
    kern = functools.partial(_pattn_kernel, n_sel=n_sel, tq=tq, rg=min(tq, 64))
    return pl.pallas_call(
        kern,
        grid=(b, nq),
        in_specs=[qblk(iqm.shape[2]), qblk(LANES), full(LANES), qblk(qm.shape[2]), full(WIDTH_A), full(WIDTH_A)],
        out_specs=qblk(WIDTH_A),
        out_shape=jax.ShapeDtypeStruct((b, t, WIDTH_A), F32),
        scratch_shapes=[
            pltpu.VMEM((nq, tq, tq), I32),
            pltpu.VMEM((tq, LANES), I32), pltpu.VMEM((tq, LANES), F32), pltpu.VMEM((tq, LANES), F32),
            pltpu.VMEM((N_HEADS_A, tq, LANES), F32), pltpu.VMEM((N_HEADS_A, tq, LANES), F32),
            pltpu.VMEM((N_HEADS_A, tq, LANES), F32),
        ],
        compiler_params=_params("parallel", "arbitrary"),
        name="pattn",
    )(iqm, ikw, ik2, qm, kb, vb)


def _hgrn_kernel(q_ref, k_ref, g_ref, v_ref, s0_ref, o_ref, sn_ref, s_scr, *, tc):
    ti = pl.program_id(1)
    c_ = CHUNK_B

    @pl.when(ti == 0)
    def _():
        for h in range(N_HEADS_B):
            s_scr[h] = s0_ref[0, h].T

    rowi = lax.broadcasted_iota(I32, (c_, LANES), 0)

    def chunk(ci, carry):
        rs = pl.ds(pl.multiple_of(ci * c_, c_), c_)
        for h in range(N_HEADS_B):
            hs = slice(h * LANES, (h + 1) * LANES)
            q = q_ref[0, rs, hs]
            k = k_ref[0, rs, hs]
            g = g_ref[0, rs, hs]
            v = v_ref[0, rs, hs]
            b = g
            for sh in (1, 2, 4, 8):
                b = b + jnp.where(rowi >= sh, pltpu.roll(b, sh, 0), 0.0)
            b_last = b[c_ - 1:c_, :]
            st_old = s_scr[h]
            o = lax.dot_general((q * jnp.exp(b)).astype(BF16), st_old.astype(BF16), NT_DIMS,
                                preferred_element_type=F32)
            for t in range(c_):
                e = jnp.exp(jnp.where(rowi <= t, b[t:t + 1, :] - b, -jnp.inf))
                att = jnp.sum(e * k * q[t:t + 1, :], axis=1, keepdims=True)
                orow = jnp.sum(att * v, axis=0, keepdims=True)
                o = o + jnp.where(rowi == t, orow, 0.0)
            o_ref[0, rs, hs] = o
            kd = k * jnp.exp(b_last - b)
            upd = jnp.dot(v.T.astype(BF16), kd.astype(BF16), preferred_element_type=F32)
            s_scr[h] = st_old * jnp.exp(b_last) + upd
        return carry

    lax.fori_loop(0, tc // c_, chunk, 0)

    @pl.when(ti == pl.num_programs(1) - 1)
    def _():
        for h in range(N_HEADS_B):
            sn_ref[0, h] = s_scr[h].T


def _hgrn(q, k, g, v, s0, tc):
    b, t, w = q.shape
    blk = pl.BlockSpec((1, tc, w), lambda bi, ti: (bi, ti, 0))
    st = pl.BlockSpec((1, N_HEADS_B, KEY_DIM_B, VAL_DIM_B), lambda bi, ti: (bi, 0, 0, 0))
    return pl.pallas_call(
        functools.partial(_hgrn_kernel, tc=tc),
        grid=(b, t // tc),
        in_specs=[blk, blk, blk, blk, st],
        out_specs=[blk, st],
        out_shape=[jax.ShapeDtypeStruct((b, t, w), F32), jax.ShapeDtypeStruct(s0.shape, F32)],
        scratch_shapes=[pltpu.VMEM((N_HEADS_B, KEY_DIM_B, VAL_DIM_B), F32)],
        compiler_params=_params("parallel", "arbitrary"),
        name="hgrn",
    )(q, k, g, v, s0)


def _merge_kernel(attn_ref, o_ref, hg_ref, ga_ref, gb_ref, x_ref, on_ref, wpa_ref, wph_ref, wo_ref, y_ref):
    o = o_ref[...]
    parts = []
    for h in range(N_HEADS_B):
        oh = o[:, h * LANES:(h + 1) * LANES]
        ms = jnp.mean(oh * oh, axis=-1, keepdims=True)
        parts.append(oh * lax.rsqrt(ms + EPS))
    on = jnp.concatenate(parts, axis=1) * on_ref[...] * hg_ref[...]
    pa = jnp.dot(attn_ref[...].astype(BF16), wpa_ref[...], preferred_element_type=F32)
    ph = jnp.dot(on.astype(BF16), wph_ref[...], preferred_element_type=F32)
    merged = ga_ref[...] * pa + gb_ref[...] * ph
    y_ref[...] = x_ref[...] + jnp.dot(merged.astype(BF16), wo_ref[...], preferred_element_type=F32)


def _merge(attn, o, hg, ga, gb, x, onorm, wpa, wph, wo, tm):
    n, d = x.shape
    row = lambda w: pl.BlockSpec((tm, w), lambda i: (i, 0))
    const = lambda r, w: pl.BlockSpec((r, w), lambda i: (0, 0))
    return pl.pallas_call(
        _merge_kernel,
        grid=(n // tm,),
        in_specs=[row(WIDTH_A), row(WIDTH_B), row(WIDTH_B), row(d), row(d), row(d),
                  const(1, WIDTH_B), const(WIDTH_A, d), const(WIDTH_B, d), const(d, d)],
        out_specs=row(d),
        out_shape=jax.ShapeDtypeStruct((n, d), F32),
        compiler_params=_params("parallel"),
        name="merge",
    )(attn, o, hg, ga, gb, x, onorm, wpa, wph, wo)


def _rot_tables(pos):
    r = pos.shape[0]
    inv = ROPE_THETA ** (-jnp.arange(0, ROT_DIM, 2, dtype=F32) / ROT_DIM)
    ang = pos.astype(F32)[:, None] * inv[None, :]
    cos, sin = jnp.cos(ang), jnp.sin(ang)
    half = ROT_DIM // 2
    rest = HEAD_DIM_A - ROT_DIM
    one, z8, zr = jnp.ones((r, rest), F32), jnp.zeros((r, half), F32), jnp.zeros((r, rest), F32)
    c64 = jnp.concatenate([cos, cos, one], axis=1)
    a64 = jnp.concatenate([-sin, z8, zr], axis=1)
    b64 = jnp.concatenate([z8, sin, zr], axis=1)
    return tuple(jnp.concatenate([t, t], axis=1) for t in (c64, a64, b64))


def _prep_weights(ffn1_norm, ffn1_w_gate, ffn1_w_up, ffn1_w_down, mix_norm, w_in, q_norm, k_norm,
                  hgrn_lb, hgrn_o_norm, w_proj_attn, w_proj_hgrn, w_out,
                  ffn2_norm, ffn2_w_gate, ffn2_w_up, ffn2_w_down):
    l = 0
    q, k, v, iq, ik, iw, hq, hf, hi, hg, ga, gb = jnp.split(w_in[l], IN_OFFSETS, axis=1)
    pad = jnp.zeros((D_MODEL, IN_CHUNK - iq.shape[1] - ik.shape[1] - iw.shape[1]), F32)
    idx = jnp.concatenate([iq, ik, iw, pad], axis=1)
    w12 = jnp.stack([q, k, v, idx, hq, hf, hi, hg,
                     ga[:, :IN_CHUNK], ga[:, IN_CHUNK:], gb[:, :IN_CHUNK], gb[:, IN_CHUNK:]]).astype(BF16)
    head = np.arange(WIDTH_A) // HEAD_DIM_A
    bd = jnp.asarray((head[:, None] == head[None, :]).astype(np.float32) / HEAD_DIM_A, BF16)
    lb = jnp.cumsum(jax.nn.softmax(hgrn_lb.astype(F32), axis=0), axis=0)[l].reshape(1, KEYW_B)
    return dict(
        ffn1=(ffn1_norm[l], ffn1_w_gate[l].astype(BF16), ffn1_w_up[l].astype(BF16), ffn1_w_down[l].astype(BF16)),
        ffn2=(ffn2_norm[l], ffn2_w_gate[l].astype(BF16), ffn2_w_up[l].astype(BF16), ffn2_w_down[l].astype(BF16)),
        mix_norm=mix_norm[l], w12=w12, bd=bd,
        qg=jnp.tile(q_norm[l], N_HEADS_A).reshape(1, WIDTH_A),
        kg=jnp.tile(k_norm[l], N_HEADS_A).reshape(1, WIDTH_A),
        lb=lb,
        onorm=jnp.tile(hgrn_o_norm[l], N_HEADS_B).reshape(1, WIDTH_B),
        wpa=w_proj_attn[l].astype(BF16), wph=w_proj_hgrn[l].astype(BF16), wo=w_out[l].astype(BF16),
    )


def _pre(x, w, tabs, tm_ffn, tm_in):
    x1 = _ffn(x, *w["ffn1"], tm_ffn)
    names = ("qm", "k", "kb", "v", "vb", "iqm", "ikw", "ik2", "hq", "kf", "lf", "hi", "hg", "ga", "gb")
    outs = _inproj(x1, w["mix_norm"], w["w12"], w["bd"], w["qg"], w["kg"], w["lb"], *tabs, tm_in)
    return x1, dict(zip(names, outs))


def _post(x1, attn, o, p, w, tm_merge, tm_ffn):
    x2 = _merge(attn, o, p["hg"], p["ga"], p["gb"], x1, w["onorm"], w["wpa"], w["wph"], w["wo"], tm_merge)
    return _ffn(x2, *w["ffn2"], tm_ffn)


def _prompt_layer(x_prompt, w):
    b, t, d = x_prompt.shape
    n = b * t
    tabs = _rot_tables(jnp.arange(t, dtype=I32))
    x1, p = _pre(x_prompt.reshape(n, d), w, tabs, min(1024, n), min(512, t))
    r3 = lambda a: a.reshape(b, t, a.shape[-1])
    n_sel = min(TOPK_MAX, t // 4)
    attn = _prompt_attention(r3(p["iqm"]), r3(p["ikw"]), r3(p["ik2"]), r3(p["qm"]), r3(p["kb"]), r3(p["vb"]),
                             n_sel, min(256, t))
    s0 = jnp.zeros((b, N_HEADS_B, KEY_DIM_B, VAL_DIM_B), F32)
    o, s_new = _hgrn(r3(p["hq"]), r3(p["kf"]), r3(p["lf"]), r3(p["hi"]), s0, min(256, t))
    y = _post(x1, attn.reshape(n, WIDTH_A), o.reshape(n, WIDTH_B), p, w, min(512, n), min(1024, n))
    return (y.reshape(b, t, d),
            p["k"].reshape(b, t, N_HEADS_A, HEAD_DIM_A), p["v"].reshape(b, t, N_HEADS_A, HEAD_DIM_A),
            p["ikw"][:, :IDX_DIM].reshape(b, t, IDX_DIM), s_new)


IDX_PAGES_PER_STEP = 16
KV_PAGES_PER_STEP = 8


def _sidx_kernel(pt_ref, iq2_ref, w2_ref, iknew_ref, *rest, pps, n_pages, n_sel):
    del pt_ref
    pages = rest[:pps]
    keys_ref, thr_ref, need_scr, flag_scr = rest[pps:]
    s = pl.program_id(1)
    nblk = keys_ref.shape[1]
    iq2 = iq2_ref[0]
    w2 = w2_ref[0] * IDX_W_SCALE
    tq = iq2.shape[0] // N_HEADS_IDX

    def block_scores(ik):
        d = lax.dot_general(iq2, ik, NT_DIMS, preferred_element_type=F32)
        r = jnp.maximum(d, 0.0) * w2
        sc = r[0:tq]
        for h in range(1, N_HEADS_IDX):
            sc = sc + r[h * tq:(h + 1) * tq]
        return sc

    for j in range(pps):
        keys_ref[0, s * pps + j] = _float_key(block_scores(pages[j][...].astype(BF16)))

    @pl.when(s == pl.num_programs(1) - 1)
    def _():
        lane = lax.broadcasted_iota(I32, (tq, LANES), 1)
        row = lax.broadcasted_iota(I32, (tq, LANES), 0)
        keys_ref[0, n_pages] = jnp.where(lane <= row, _float_key(block_scores(iknew_ref[0])), INT_MIN)
        for u in range(n_pages + 1, nblk):
            keys_ref[0, u] = jnp.full((tq, LANES), INT_MIN, I32)

        def count_ge(t):
            def body(g, acc):
                kk = keys_ref[0, pl.ds(pl.multiple_of(g * SUBLANES, SUBLANES), SUBLANES)]
                for u in range(SUBLANES):
                    acc = acc + jnp.where(kk[u] >= t, 1.0, 0.0)
                return acc
            acc = lax.fori_loop(0, nblk // SUBLANES, body, jnp.zeros((tq, LANES), F32))
            return jnp.sum(acc, axis=1, keepdims=True)

        cand = _search_threshold(count_ge, tq, n_sel)
        cge = count_ge(cand)
        cgt = count_ge(cand + 1)
        thr_ref[0] = jnp.broadcast_to(jnp.maximum(cand, INT_MIN + 1), (tq, LANES))
        need_scr[...] = jnp.broadcast_to(n_sel - cgt, (tq, LANES))
        flag_scr[...] = jnp.broadcast_to(
            jnp.where(cge > n_sel, jnp.where(cand > INT_MIN, 1.0, 0.0), 0.0), (tq, LANES))

        @pl.when(jnp.max(flag_scr[...]) > 0.0)
        def _():
            need = need_scr[:, :1]
            r2 = lax.broadcasted_iota(I32, (LANES, LANES), 0)
            c2 = lax.broadcasted_iota(I32, (LANES, LANES), 1)
            upper = jnp.where(r2 < c2, 1.0, 0.0)

            def tie_blk(u, seen):
                kk = keys_ref[0, u]
                tf = jnp.where(kk == cand, 1.0, 0.0)
                rank = jnp.dot(tf, upper, preferred_element_type=F32) + seen
                drop = jnp.where(rank >= need, tf, 0.0)
                keys_ref[0, u] = jnp.where(drop > 0.0, INT_MIN, kk)
                return seen + jnp.sum(tf, axis=1, keepdims=True)

            lax.fori_loop(0, n_pages + 1, tie_blk, jnp.zeros((tq, 1), F32))


def _sample_index(page_table, iq2, w2, iknew, idx_pages, n_sel):
    b, n_pages = page_table.shape
    pps = min(IDX_PAGES_PER_STEP, n_pages)
    tq = iq2.shape[1] // N_HEADS_IDX
    nblk = -(-(n_pages + 1) // SUBLANES) * SUBLANES
    per_b = lambda r, w: pl.BlockSpec((1, r, w), lambda bi, s, pt: (bi, 0, 0))
    page = lambda j: pl.BlockSpec((None, PAGE_SIZE, IDX_DIM), lambda bi, s, pt, j=j: (pt[bi, s * pps + j], 0, 0))
    return pl.pallas_call(
        functools.partial(_sidx_kernel, pps=pps, n_pages=n_pages, n_sel=n_sel),
        grid_spec=pltpu.PrefetchScalarGridSpec(
            num_scalar_prefetch=1,
            grid=(b, n_pages // pps),
            in_specs=[per_b(N_HEADS_IDX * tq, IDX_DIM), per_b(N_HEADS_IDX * tq, LANES), per_b(PAGE_SIZE, IDX_DIM)]
            + [page(j) for j in range(pps)],
            out_specs=[pl.BlockSpec((1, nblk, tq, LANES), lambda bi, s, pt: (bi, 0, 0, 0)),
                       pl.BlockSpec((1, tq, LANES), lambda bi, s, pt: (bi, 0, 0))],
            scratch_shapes=[pltpu.VMEM((tq, LANES), F32), pltpu.VMEM((tq, LANES), F32)],
        ),
        out_shape=[jax.ShapeDtypeStruct((b, nblk, tq, LANES), I32), jax.ShapeDtypeStruct((b, tq, LANES), I32)],
        compiler_params=_params("parallel", "arbitrary"),
        name="sidx",
    )(page_table, iq2, w2, iknew, *([idx_pages] * pps))


def _sattn_kernel(pt_ref, q2_ref, keys_ref, thr_ref, knew_ref, vnew_ref, *rest, ppb, n_pages):
    del pt_ref
    kpages, vpages = rest[:ppb], rest[ppb:2 * ppb]
    o_ref, m_scr, l_scr, acc_scr = rest[2 * ppb:]
    s = pl.program_id(1)
    nh = N_HEADS_A
    q2 = q2_ref[0]
    tq = q2.shape[0] // nh
    thr = jnp.concatenate([thr_ref[0]] * nh, axis=0)
    rb = lax.broadcasted_iota(I32, (nh * tq, nh * PAGE_SIZE), 0) // tq
    cb = lax.broadcasted_iota(I32, (nh * tq, nh * PAGE_SIZE), 1) // PAGE_SIZE
    same_head = rb == cb

    @pl.when(s == 0)
    def _():
        m_scr[...] = jnp.full(m_scr.shape, NEG_BIG, F32)
        l_scr[...] = jnp.zeros(l_scr.shape, F32)
        acc_scr[...] = jnp.zeros(acc_scr.shape, F32)

    def by_head(ref):
        return jnp.concatenate(
            [ref[pl.ds(h, PAGE_SIZE, stride=nh), :].astype(BF16) for h in range(nh)], axis=0)

    def page_update(kref, vref, blk):
        lg = lax.dot_general(q2, by_head(kref), NT_DIMS, preferred_element_type=F32)
        sc = jnp.concatenate([lg[h * tq:(h + 1) * tq, h * PAGE_SIZE:(h + 1) * PAGE_SIZE] for h in range(nh)],
                             axis=0)
        kk = jnp.concatenate([keys_ref[0, blk]] * nh, axis=0)
        sc = jnp.where(kk >= thr, sc, NEG_BIG)
        m_old = m_scr[:, :1]
        m_new = jnp.maximum(m_old, jnp.max(sc, axis=1, keepdims=True))
        alpha = jnp.exp(m_old - m_new)
        p = jnp.exp(sc - m_new)
        l_scr[...] = alpha * l_scr[...] + jnp.sum(p, axis=1, keepdims=True)
        p8 = jnp.where(same_head, jnp.concatenate([p] * nh, axis=1), 0.0).astype(BF16)
        acc_scr[...] = alpha * acc_scr[...] + jnp.dot(p8, by_head(vref), preferred_element_type=F32)
        m_scr[...] = jnp.broadcast_to(m_new, m_scr.shape)

    for j in range(ppb):
        page_update(kpages[j], vpages[j], s * ppb + j)

    @pl.when(s == pl.num_programs(1) - 1)
    def _():
        page_update(knew_ref.at[0], vnew_ref.at[0], n_pages)
        o_ref[0] = acc_scr[...] / l_scr[:, :HEAD_DIM_A]


def _sample_attention(page_table, q2, keys, thr, knew, vnew, k_pages, v_pages):
    b, n_pages = page_table.shape
    ppb = min(KV_PAGES_PER_STEP, n_pages)
    rows = q2.shape[1]
    prow = PAGE_SIZE * N_HEADS_A
    per_b3 = lambda r, w: pl.BlockSpec((1, r, w), lambda bi, s, pt: (bi, 0, 0))
    page = lambda j: pl.BlockSpec((None, prow, HEAD_DIM_A), lambda bi, s, pt, j=j: (pt[bi, s * ppb + j], 0, 0))
    return pl.pallas_call(
        functools.partial(_sattn_kernel, ppb=ppb, n_pages=n_pages),
        grid_spec=pltpu.PrefetchScalarGridSpec(
            num_scalar_prefetch=1,
            grid=(b, n_pages // ppb),
            in_specs=[per_b3(rows, HEAD_DIM_A),
                      pl.BlockSpec((1,) + keys.shape[1:], lambda bi, s, pt: (bi, 0, 0, 0)),
                      per_b3(thr.shape[1], LANES), per_b3(prow, HEAD_DIM_A), per_b3(prow, HEAD_DIM_A)]
            + [page(j) for j in range(ppb)] * 2,
            out_specs=per_b3(rows, HEAD_DIM_A),
            scratch_shapes=[pltpu.VMEM((rows, LANES), F32), pltpu.VMEM((rows, LANES), F32),
                            pltpu.VMEM((rows, HEAD_DIM_A), F32)],
        ),
        out_shape=jax.ShapeDtypeStruct((b, rows, HEAD_DIM_A), F32),
        compiler_params=_params("parallel", "arbitrary"),
        name="sattn",
    )(page_table, q2, keys, thr, knew, vnew, *([k_pages] * ppb), *([v_pages] * ppb))


def _unmask_heads(xm, n_heads):
    x = xm.reshape(xm.shape[0], n_heads, LANES)
    return jnp.stack([x[:, h, (h % 2) * HEAD_DIM_A:(h % 2 + 1) * HEAD_DIM_A] for h in range(n_heads)], axis=1)


def _sample_layer(x_sample, cache_k, cache_v, cache_idx_k, state, page_table, w):
    b, t, d = x_sample.shape
    n = b * t
    n_pages = page_table.shape[1]
    past = n_pages * PAGE_SIZE
    pos = past + jnp.arange(t, dtype=I32)
    x1, p = _pre(x_sample.reshape(n, d), w, _rot_tables(jnp.tile(pos, b)), n, n)

    iq2 = _unmask_heads(p["iqm"], N_HEADS_IDX).reshape(b, t, N_HEADS_IDX, IDX_DIM)
    iq2 = iq2.transpose(0, 2, 1, 3).reshape(b, N_HEADS_IDX * t, IDX_DIM)
    w2 = p["ikw"][:, IDX_DIM:IDX_DIM + N_HEADS_IDX].reshape(b, t, N_HEADS_IDX).transpose(0, 2, 1)
    w2 = jnp.broadcast_to(w2.reshape(b, N_HEADS_IDX * t, 1), (b, N_HEADS_IDX * t, LANES))
    iknew = jnp.pad(p["ik2"][:, :IDX_DIM].reshape(b, t, IDX_DIM), ((0, 0), (0, PAGE_SIZE - t), (0, 0)))
    n_sel = min(TOPK_MAX, (past + t) // 4)
    keys, thr = _sample_index(page_table, iq2, w2, iknew, cache_idx_k[0], n_sel)

    q2 = _unmask_heads(p["qm"], N_HEADS_A).reshape(b, t, N_HEADS_A, HEAD_DIM_A)
    q2 = q2.transpose(0, 2, 1, 3).reshape(b, N_HEADS_A * t, HEAD_DIM_A)
    prow = PAGE_SIZE * N_HEADS_A
    newpage = lambda a: jnp.pad(a.reshape(b, t * N_HEADS_A, HEAD_DIM_A), ((0, 0), (0, prow - t * N_HEADS_A), (0, 0)))
    o2 = _sample_attention(page_table, q2, keys, thr, newpage(p["k"]), newpage(p["v"]),
                           cache_k[0].reshape(-1, prow, HEAD_DIM_A), cache_v[0].reshape(-1, prow, HEAD_DIM_A))
    attn = o2.reshape(b, N_HEADS_A, t, HEAD_DIM_A).transpose(0, 2, 1, 3).reshape(n, WIDTH_A)

    tp = -(-t // CHUNK_B) * CHUNK_B
    r3 = lambda a: jnp.pad(a.reshape(b, t, a.shape[-1]), ((0, 0), (0, tp - t), (0, 0)))
    o, s_new = _hgrn(r3(p["hq"]), r3(p["kf"]), r3(p["lf"]), r3(p["hi"]), state, tp)
    y = _post(x1, attn, o[:, :t].reshape(n, WIDTH_B), p, w, n, n)
    return (y.reshape(b, t, d),
            p["k"].reshape(b, t, N_HEADS_A, HEAD_DIM_A), p["v"].reshape(b, t, N_HEADS_A, HEAD_DIM_A),
            p["ikw"][:, :IDX_DIM].reshape(b, t, IDX_DIM), s_new)


def kernel(x_prompt, x_sample, cache_k, cache_v, cache_idx_k, state_hgrn, page_table,
           ffn1_norm, ffn1_w_gate, ffn1_w_up, ffn1_w_down, mix_norm, w_in, q_norm, k_norm,
           hgrn_lb, hgrn_o_norm, w_proj_attn, w_proj_hgrn, w_out,
           ffn2_norm, ffn2_w_gate, ffn2_w_up, ffn2_w_down):
    w = _prep_weights(ffn1_norm, ffn1_w_gate, ffn1_w_up, ffn1_w_down, mix_norm, w_in, q_norm, k_norm,
                      hgrn_lb, hgrn_o_norm, w_proj_attn, w_proj_hgrn, w_out,
                      ffn2_norm, ffn2_w_gate, ffn2_w_up, ffn2_w_down)
    yp, kp, vp, ikp, sp = _prompt_layer(x_prompt, w)
    ys, ks, vs, iks, ss = _sample_layer(x_sample, cache_k, cache_v, cache_idx_k, state_hgrn[0], page_table, w)
    return (yp, ys, kp[None], vp[None], ikp[None], sp[None], ks[None], vs[None], iks[None], ss[None])
```

```python
import functools

import numpy as np
import jax
import jax.numpy as jnp
from jax import lax
from jax.experimental import pallas as pl
from jax.experimental.pallas import tpu as pltpu

F32 = jnp.float32
BF16 = jnp.bfloat16
I32 = jnp.int32

D_MODEL = 1024
PAST_LEN = 16384
PAGE_SIZE = 128
N_HEADS_A = 8
HEAD_DIM_A = 64
ROT_DIM = HEAD_DIM_A // 4
ROPE_THETA = 500000.0
N_HEADS_IDX = 4
IDX_DIM = 64
IDX_W_SCALE = (N_HEADS_IDX * IDX_DIM) ** -0.5
TOPK_MAX = 256
N_HEADS_B = 4
KEY_DIM_B = 128
VAL_DIM_B = 128
CHUNK_B = 16
WIDTH_A = N_HEADS_A * HEAD_DIM_A
WIDTH_B = N_HEADS_B * VAL_DIM_B
KEYW_B = N_HEADS_B * KEY_DIM_B
EPS = 1e-6
IN_SIZES = (WIDTH_A, WIDTH_A, WIDTH_A, N_HEADS_IDX * IDX_DIM, IDX_DIM, N_HEADS_IDX,
            KEYW_B, KEYW_B, WIDTH_B, WIDTH_B, D_MODEL, D_MODEL)
IN_OFFSETS = tuple(int(o) for o in np.cumsum(IN_SIZES)[:-1])

LANES = 128
SUBLANES = 8
VMEM_LIMIT_BYTES = 56 * 1024 * 1024
INT_MIN = np.int32(-2 ** 31)
NEG_BIG = -1e30
FF_CHUNK = 256
IN_CHUNK = 512
NT_DIMS = (((1,), (1,)), ((), ()))
LOG2E = 1.4426950408889634
ATTN_TILE = 256


def _params(*sem):
    return pltpu.CompilerParams(dimension_semantics=sem, vmem_limit_bytes=VMEM_LIMIT_BYTES)


def _float_key(x):
    u = pltpu.bitcast(x + 0.0, I32)
    return u ^ ((u >> 31) & np.int32(0x7FFFFFFF))


def _ffn_kernel(x_ref, g_ref, wg_ref, wu_ref, wd_ref, o_ref, h_scr, acc_scr):
    j = pl.program_id(1)

    @pl.when(j == 0)
    def _():
        x = x_ref[...]
        ms = jnp.mean(x * x, axis=-1, keepdims=True)
        h_scr[...] = (x * lax.rsqrt(ms + EPS) * g_ref[...]).astype(BF16)
        acc_scr[...] = jnp.zeros_like(acc_scr)

    h = h_scr[...]
    g = jnp.dot(h, wg_ref[...], preferred_element_type=F32)
    u = jnp.dot(h, wu_ref[...], preferred_element_type=F32)
    a = (g * jax.nn.sigmoid(g) * u).astype(BF16)
    acc_scr[...] += jnp.dot(a, wd_ref[...], preferred_element_type=F32)

    @pl.when(j == pl.num_programs(1) - 1)
    def _():
        o_ref[...] = x_ref[...] + 0.5 * acc_scr[...]


def _ffn(x, gain, wg, wu, wd, tm):
    n, d = x.shape
    ff = wg.shape[1]
    return pl.pallas_call(
        _ffn_kernel,
        grid=(n // tm, ff // FF_CHUNK),
        in_specs=[
            pl.BlockSpec((tm, d), lambda i, j: (i, 0)),
            pl.BlockSpec((1, d), lambda i, j: (0, 0)),
            pl.BlockSpec((d, FF_CHUNK), lambda i, j: (0, j)),
            pl.BlockSpec((d, FF_CHUNK), lambda i, j: (0, j)),
            pl.BlockSpec((FF_CHUNK, d), lambda i, j: (j, 0)),
        ],
        out_specs=pl.BlockSpec((tm, d), lambda i, j: (i, 0)),
        out_shape=jax.ShapeDtypeStruct((n, d), F32),
        scratch_shapes=[pltpu.VMEM((tm, d), BF16), pltpu.VMEM((tm, d), F32)],
        compiler_params=_params("parallel", "arbitrary"),
        name="ffn",
    )(x, gain.reshape(1, d), wg, wu, wd)


_J_Q, _J_K, _J_V, _J_IDX, _J_HQ, _J_HF, _J_HI, _J_HG, _J_GA0, _J_GA1, _J_GB0, _J_GB1 = range(12)


def _rot(xc, c, sa, sb):
    return (xc * c + pltpu.roll(xc, LANES - ROT_DIM // 2, 1) * sa
            + pltpu.roll(xc, ROT_DIM // 2, 1) * sb)


def _inproj_kernel(x_ref, g_ref, w_ref, bd_ref, qg_ref, kg_ref, lb_ref, cos_ref, sa_ref, sb_ref,
                   qm_ref, k_ref, kb_ref, v_ref, vt_ref, iqm_ref, ikw_ref, ik2_ref,
                   hq_ref, kf_ref, lf_ref, hi_ref, hg_ref, ga_ref, gb_ref, h_scr):
    j = pl.program_id(1)
    tm = x_ref.shape[0]

    @pl.when(j == 0)
    def _():
        x = x_ref[...]
        ms = jnp.mean(x * x, axis=-1, keepdims=True)
        h_scr[...] = (x * lax.rsqrt(ms + EPS) * g_ref[...]).astype(BF16)

    y = jnp.dot(h_scr[...], w_ref[0], preferred_element_type=F32)
    lane = lax.broadcasted_iota(I32, (tm, LANES), 1)
    lo_half = lane < HEAD_DIM_A

    def head_norm(t, gain):
        sq = t * t
        hi = sq.astype(BF16)
        lo = (sq - hi.astype(F32)).astype(BF16)
        ms = (jnp.dot(hi, bd_ref[...], preferred_element_type=F32)
              + jnp.dot(lo, bd_ref[...], preferred_element_type=F32))
        return t * lax.rsqrt(ms + EPS) * gain

    @pl.when(j == _J_Q)
    def _():
        yn = head_norm(y, qg_ref[...])
        c, sa, sb = cos_ref[...], sa_ref[...], sb_ref[...]
        for p in range(WIDTH_A // LANES):
            r = _rot(yn[:, p * LANES:(p + 1) * LANES], c, sa, sb) * (HEAD_DIM_A ** -0.5 * LOG2E)
            qm_ref[:, (2 * p) * LANES:(2 * p + 1) * LANES] = jnp.where(lo_half, r, 0.0).astype(BF16)
            qm_ref[:, (2 * p + 1) * LANES:(2 * p + 2) * LANES] = jnp.where(lo_half, 0.0, r).astype(BF16)

    @pl.when(j == _J_K)
    def _():
        yn = head_norm(y, kg_ref[...])
        c, sa, sb = cos_ref[...], sa_ref[...], sb_ref[...]
        for p in range(WIDTH_A // LANES):
            sl = slice(p * LANES, (p + 1) * LANES)
            r = _rot(yn[:, sl], c, sa, sb)
            k_ref[:, sl] = r
            kb_ref[:, sl] = r.astype(BF16)

    @pl.when(j == _J_V)
    def _():
        v_ref[...] = y
        for u in range(vt_ref.shape[0]):
            vt_ref[u] = y[u * vt_ref.shape[2]:(u + 1) * vt_ref.shape[2], :].T.astype(BF16)

    @pl.when(j == _J_IDX)
    def _():
        c, sa, sb = cos_ref[...], sa_ref[...], sb_ref[...]
        for p in range(2):
            r = _rot(y[:, p * LANES:(p + 1) * LANES], c, sa, sb)
            iqm_ref[:, (2 * p) * LANES:(2 * p + 1) * LANES] = jnp.where(lo_half, r, 0.0).astype(BF16)
            iqm_ref[:, (2 * p + 1) * LANES:(2 * p + 2) * LANES] = jnp.where(lo_half, 0.0, r).astype(BF16)
        r = _rot(y[:, 2 * LANES:3 * LANES], jnp.where(lo_half, c, 1.0),
                 jnp.where(lo_half, sa, 0.0), jnp.where(lo_half, sb, 0.0))
        ikw_ref[...] = r
        ik2_ref[...] = jnp.where(lo_half, r, pltpu.roll(r, HEAD_DIM_A, 1)).astype(BF16)

    @pl.when(j == _J_HQ)
    def _():
        hq_ref[...] = y * jax.nn.sigmoid(y)

    @pl.when(j == _J_HF)
    def _():
        lb = lb_ref[...]
        lf_ref[...] = jnp.log(lb + (1.0 - lb) * jax.nn.sigmoid(y))
        kf_ref[...] = (1.0 - lb) * jax.nn.sigmoid(-y)

    @pl.when(j == _J_HI)
    def _():
        hi_ref[...] = y

    @pl.when(j == _J_HG)
    def _():
        hg_ref[...] = y * jax.nn.sigmoid(y)

    for jj, ref, half in ((_J_GA0, ga_ref, 0), (_J_GA1, ga_ref, 1), (_J_GB0, gb_ref, 0), (_J_GB1, gb_ref, 1)):
        @pl.when(j == jj)
        def _(ref=ref, half=half):
            ref[:, half * IN_CHUNK:(half + 1) * IN_CHUNK] = jax.nn.sigmoid(y)


def _inproj(x, gain, w12, bd, qg, kg, lb, cos_t, sa_t, sb_t, tm):
    n, d = x.shape
    nt = cos_t.shape[0] // tm
    row = lambda w: pl.BlockSpec((tm, w), lambda i, j: (i, 0))
    const = lambda r, w: pl.BlockSpec((r, w), lambda i, j: (0, 0))
    tab = pl.BlockSpec((tm, LANES), lambda i, j: (i % nt, 0))
    out_widths = [(2 * WIDTH_A, BF16), (WIDTH_A, F32), (WIDTH_A, BF16), (WIDTH_A, F32), None,
                  (2 * N_HEADS_IDX * IDX_DIM, BF16), (LANES, F32), (LANES, BF16),
                  (KEYW_B, F32), (KEYW_B, F32), (KEYW_B, F32), (WIDTH_B, F32), (WIDTH_B, F32),
                  (D_MODEL, F32), (D_MODEL, F32)]
    vt_tile = min(ATTN_TILE, tm)
    vt_spec = pl.BlockSpec((tm // vt_tile, WIDTH_A, vt_tile), lambda i, j: (i, 0, 0))
    vt_shape = jax.ShapeDtypeStruct((n // vt_tile, WIDTH_A, vt_tile), BF16)
    return pl.pallas_call(
        _inproj_kernel,
        grid=(n // tm, w12.shape[0]),
        in_specs=[
            row(d), const(1, d),
            pl.BlockSpec((1, d, IN_CHUNK), lambda i, j: (j, 0, 0)),
            const(WIDTH_A, WIDTH_A), const(1, WIDTH_A), const(1, WIDTH_A), const(1, KEYW_B),
            tab, tab, tab,
        ],
        out_specs=[vt_spec if o is None else row(o[0]) for o in out_widths],
        out_shape=[vt_shape if o is None else jax.ShapeDtypeStruct((n, o[0]), o[1]) for o in out_widths],
        scratch_shapes=[pltpu.VMEM((tm, d), BF16)],
        compiler_params=_params("parallel", "arbitrary"),
        name="inproj",
    )(x, gain.reshape(1, d), w12, bd, qg, kg, lb, cos_t, sa_t, sb_t)


def _tree_sum(parts):
    while len(parts) > 1:
        parts = [a + b for a, b in zip(parts[0::2], parts[1::2])] + ([parts[-1]] if len(parts) % 2 else [])
    return parts[0]


def _search_threshold(count_ge, shape, n_sel):
    c0 = count_ge(jnp.zeros(shape, I32))
    ok = c0 >= n_sel
    cand = jnp.where(ok, np.int32(0), INT_MIN)
    ccnt = jnp.where(ok, c0, 0.0)

    def bit_body(i, carry):
        cand, ccnt = carry
        t = cand + (jnp.int32(1) << (30 - i))
        c = count_ge(t)
        ok = c >= n_sel
        return jnp.where(ok, t, cand), jnp.where(ok, c, ccnt)

    return lax.fori_loop(0, 31, bit_body, (cand, ccnt))


def _pattn_kernel(iqm_ref, ikw_ref, ik2_ref, qm_ref, k_ref, vt_ref, o_ref,
                  keys_scr, m_scr, a_scr, acc_scr, s_scr, p_scr, *, n_sel, tq):
    qi = pl.program_id(1)
    nkb = qi + 1
    krow = lax.broadcasted_iota(I32, (tq, tq), 0)
    qcol = lax.broadcasted_iota(I32, (tq, tq), 1)
    half = HEAD_DIM_A

    ikw_t = ikw_ref[0].T
    wrow = [ikw_t[IDX_DIM + h:IDX_DIM + h + 1, :] * IDX_W_SCALE for h in range(N_HEADS_IDX)]

    def score_blk(kb, c):
        ik2 = ik2_ref[0, pl.ds(pl.multiple_of(kb * tq, tq), tq), :]
        sc = jnp.zeros((tq, tq), F32)
        for h in range(N_HEADS_IDX):
            d = lax.dot_general(ik2, iqm_ref[0, :, h * LANES:(h + 1) * LANES], NT_DIMS,
                                preferred_element_type=F32)
            sc = sc + jnp.maximum(d, 0.0) * wrow[h]
        off = jnp.where(kb < qi, jnp.int32(tq), jnp.int32(0))
        keys_scr[kb] = jnp.where(krow <= qcol + off, _float_key(sc), INT_MIN)
        return c

    lax.fori_loop(0, nkb, score_blk, 0)

    def count_ge(t):
        def body(kb, acc):
            kk = keys_scr[kb]
            parts = [jnp.where(kk[r * SUBLANES:(r + 1) * SUBLANES, :] >= t, 1.0, 0.0)
                     for r in range(tq // SUBLANES)]
            return acc + _tree_sum(parts)
        acc = lax.fori_loop(0, nkb, body, jnp.zeros((SUBLANES, tq), F32))
        return jnp.sum(acc, axis=0, keepdims=True)

    cand, ccnt = _search_threshold(count_ge, (1, tq), n_sel)

    @pl.when(jnp.max(ccnt) > n_sel)
    def _():
        need = n_sel - count_ge(cand + 1)
        lower = jnp.where(qcol < krow, 1.0, 0.0).astype(BF16)

        def tie_blk(kb, seen):
            kk = keys_scr[kb]
            tf = jnp.where(kk == cand, 1.0, 0.0)
            rank = jnp.dot(lower, tf.astype(BF16), preferred_element_type=F32) + seen
            drop = jnp.where(rank >= need, tf, 0.0)
            keys_scr[kb] = jnp.where(drop > 0.0, INT_MIN, kk)
            return seen + jnp.sum(tf, axis=0, keepdims=True)

        lax.fori_loop(0, nkb, tie_blk, jnp.zeros((1, tq), F32))

    thr = jnp.maximum(cand, INT_MIN + 1)
    m_scr[...] = jnp.full(m_scr.shape, NEG_BIG, F32)
    acc_scr[...] = jnp.zeros(acc_scr.shape, F32)
    ones_half = jnp.ones((half, tq), BF16)

    def attn_blk(kb, c):
        bias = jnp.where(keys_scr[kb] >= thr, 0.0, NEG_BIG)
        ks = pl.ds(pl.multiple_of(kb * tq, tq), tq)
        for h in range(N_HEADS_A):
            pr = h // 2
            s_scr[h] = lax.dot_general(k_ref[0, ks, pr * LANES:(pr + 1) * LANES],
                                       qm_ref[0, :, h * LANES:(h + 1) * LANES], NT_DIMS,
                                       preferred_element_type=F32) + bias
        for h in range(N_HEADS_A):
            s = s_scr[h]
            m_old = m_scr[h]
            m_new = jnp.maximum(m_old, jnp.max(s, axis=0, keepdims=True))
            a_scr[h] = jnp.exp2(m_old - m_new)
            p_scr[h] = jnp.exp2(s - m_new).astype(BF16)
            m_scr[h] = m_new
        for h in range(N_HEADS_A):
            pr = h // 2
            vt = vt_ref[0, kb, pr * LANES:(pr + 1) * LANES, :]
            lhs = (jnp.concatenate([vt[:half], ones_half], axis=0) if h % 2 == 0
                   else jnp.concatenate([ones_half, vt[half:]], axis=0))
            acc_scr[h] = a_scr[h] * acc_scr[h] + jnp.dot(lhs, p_scr[h], preferred_element_type=F32)
        return c

    lax.fori_loop(0, nkb, attn_blk, 0)
    for pr in range(N_HEADS_A // 2):
        ae, ao = acc_scr[2 * pr], acc_scr[2 * pr + 1]
        ot = jnp.concatenate([ae[:half] / ae[half:], ao[half:] / ao[:half]], axis=0)
        o_ref[0, :, pr * LANES:(pr + 1) * LANES] = ot.T


def _prompt_attention(iqm, ikw, ik2, qm, kb, vt, n_sel, tq):
    b, t, _ = qm.shape
    nq = t // tq
    qblk = lambda w: pl.BlockSpec((1, tq, w), lambda bi, qi: (bi, qi, 0))
    full = lambda w: pl.BlockSpec((1, t, w), lambda bi, qi: (bi, 0, 0))
    return pl.pallas_call(
        functools.partial(_pattn_kernel, n_sel=n_sel, tq=tq),
        grid=(b, nq),
        in_specs=[qblk(iqm.shape[2]), qblk(LANES), full(LANES), qblk(qm.shape[2]), full(WIDTH_A),
                  pl.BlockSpec((1, nq, WIDTH_A, tq), lambda bi, qi: (bi, 0, 0, 0))],
        out_specs=qblk(WIDTH_A),
        out_shape=jax.ShapeDtypeStruct((b, t, WIDTH_A), F32),
        scratch_shapes=[
            pltpu.VMEM((nq, tq, tq), I32),
            pltpu.VMEM((N_HEADS_A, 1, tq), F32),
            pltpu.VMEM((N_HEADS_A, 1, tq), F32),
            pltpu.VMEM((N_HEADS_A, LANES, tq), F32),
            pltpu.VMEM((N_HEADS_A, tq, tq), F32),
            pltpu.VMEM((N_HEADS_A, tq, tq), BF16),
        ],
        compiler_params=_params("parallel", "arbitrary"),
        name="pattn",
    )(iqm, ikw, ik2, qm, kb, vt)


def _hgrn_kernel(q_ref, k_ref, g_ref, v_ref, s0_ref, o_ref, sn_ref, s_scr, *, tc):
    ti = pl.program_id(1)
    c_ = CHUNK_B

    @pl.when(ti == 0)
    def _():
        for h in range(N_HEADS_B):
            s_scr[h] = s0_ref[0, h].T

    rowi = lax.broadcasted_iota(I32, (c_, LANES), 0)

    def chunk(ci, carry):
        rs = pl.ds(pl.multiple_of(ci * c_, c_), c_)
        for h in range(N_HEADS_B):
            hs = slice(h * LANES, (h + 1) * LANES)
            q = q_ref[0, rs, hs]
            k = k_ref[0, rs, hs]
            g = g_ref[0, rs, hs]
            v = v_ref[0, rs, hs]
            b = g
            for sh in (1, 2, 4, 8):
                b = b + jnp.where(rowi >= sh, pltpu.roll(b, sh, 0), 0.0)
            b_last = b[c_ - 1:c_, :]
            st_old = s_scr[h]
            o = lax.dot_general((q * jnp.exp(b)).astype(BF16), st_old.astype(BF16), NT_DIMS,
                                preferred_element_type=F32)
            for t in range(c_):
                e = jnp.exp(jnp.where(rowi <= t, b[t:t + 1, :] - b, -jnp.inf))
                att = jnp.sum(e * k * q[t:t + 1, :], axis=1, keepdims=True)
                orow = jnp.sum(att * v, axis=0, keepdims=True)
                o = o + jnp.where(rowi == t, orow, 0.0)
            o_ref[0, rs, hs] = o
            kd = k * jnp.exp(b_last - b)
            upd = jnp.dot(v.T.astype(BF16), kd.astype(BF16), preferred_element_type=F32)
            s_scr[h] = st_old * jnp.exp(b_last) + upd
        return carry

    lax.fori_loop(0, tc // c_, chunk, 0)

    @pl.when(ti == pl.num_programs(1) - 1)
    def _():
        for h in range(N_HEADS_B):
            sn_ref[0, h] = s_scr[h].T


def _hgrn(q, k, g, v, s0, tc):
    b, t, w = q.shape
    blk = pl.BlockSpec((1, tc, w), lambda bi, ti: (bi, ti, 0))
    st = pl.BlockSpec((1, N_HEADS_B, KEY_DIM_B, VAL_DIM_B), lambda bi, ti: (bi, 0, 0, 0))
    return pl.pallas_call(
        functools.partial(_hgrn_kernel, tc=tc),
        grid=(b, t // tc),
        in_specs=[blk, blk, blk, blk, st],
        out_specs=[blk, st],
        out_shape=[jax.ShapeDtypeStruct((b, t, w), F32), jax.ShapeDtypeStruct(s0.shape, F32)],
        scratch_shapes=[pltpu.VMEM((N_HEADS_B, KEY_DIM_B, VAL_DIM_B), F32)],
        compiler_params=_params("parallel", "arbitrary"),
        name="hgrn",
    )(q, k, g, v, s0)


def _merge_kernel(attn_ref, o_ref, hg_ref, ga_ref, gb_ref, x_ref, on_ref, wpa_ref, wph_ref, wo_ref, y_ref):
    o = o_ref[...]
    parts = []
    for h in range(N_HEADS_B):
        oh = o[:, h * LANES:(h + 1) * LANES]
        ms = jnp.mean(oh * oh, axis=-1, keepdims=True)
        parts.append(oh * lax.rsqrt(ms + EPS))
    on = jnp.concatenate(parts, axis=1) * on_ref[...] * hg_ref[...]
    pa = jnp.dot(attn_ref[...].astype(BF16), wpa_ref[...], preferred_element_type=F32)
    ph = jnp.dot(on.astype(BF16), wph_ref[...], preferred_element_type=F32)
    merged = ga_ref[...] * pa + gb_ref[...] * ph
    y_ref[...] = x_ref[...] + jnp.dot(merged.astype(BF16), wo_ref[...], preferred_element_type=F32)


def _merge(attn, o, hg, ga, gb, x, onorm, wpa, wph, wo, tm):
    n, d = x.shape
    row = lambda w: pl.BlockSpec((tm, w), lambda i: (i, 0))
    const = lambda r, w: pl.BlockSpec((r, w), lambda i: (0, 0))
    return pl.pallas_call(
        _merge_kernel,
        grid=(n // tm,),
        in_specs=[row(WIDTH_A), row(WIDTH_B), row(WIDTH_B), row(d), row(d), row(d),
                  const(1, WIDTH_B), const(WIDTH_A, d), const(WIDTH_B, d), const(d, d)],
        out_specs=row(d),
        out_shape=jax.ShapeDtypeStruct((n, d), F32),
        compiler_params=_params("parallel"),
        name="merge",
    )(attn, o, hg, ga, gb, x, onorm, wpa, wph, wo)


def _rot_tables(pos):
    r = pos.shape[0]
    inv = ROPE_THETA ** (-jnp.arange(0, ROT_DIM, 2, dtype=F32) / ROT_DIM)
    ang = pos.astype(F32)[:, None] * inv[None, :]
    cos, sin = jnp.cos(ang), jnp.sin(ang)
    half = ROT_DIM // 2
    rest = HEAD_DIM_A - ROT_DIM
    one, z8, zr = jnp.ones((r, rest), F32), jnp.zeros((r, half), F32), jnp.zeros((r, rest), F32)
    c64 = jnp.concatenate([cos, cos, one], axis=1)
    a64 = jnp.concatenate([-sin, z8, zr], axis=1)
    b64 = jnp.concatenate([z8, sin, zr], axis=1)
    return tuple(jnp.concatenate([t, t], axis=1) for t in (c64, a64, b64))


def _prep_weights(ffn1_norm, ffn1_w_gate, ffn1_w_up, ffn1_w_down, mix_norm, w_in, q_norm, k_norm,
                  hgrn_lb, hgrn_o_norm, w_proj_attn, w_proj_hgrn, w_out,
                  ffn2_norm, ffn2_w_gate, ffn2_w_up, ffn2_w_down):
    l = 0
    q, k, v, iq, ik, iw, hq, hf, hi, hg, ga, gb = jnp.split(w_in[l], IN_OFFSETS, axis=1)
    pad = jnp.zeros((D_MODEL, IN_CHUNK - iq.shape[1] - ik.shape[1] - iw.shape[1]), F32)
    idx = jnp.concatenate([iq, ik, iw, pad], axis=1)
    w12 = jnp.stack([q, k, v, idx, hq, hf, hi, hg,
                     ga[:, :IN_CHUNK], ga[:, IN_CHUNK:], gb[:, :IN_CHUNK], gb[:, IN_CHUNK:]]).astype(BF16)
    head = np.arange(WIDTH_A) // HEAD_DIM_A
    bd = jnp.asarray((head[:, None] == head[None, :]).astype(np.float32) / HEAD_DIM_A, BF16)
    lb = jnp.cumsum(jax.nn.softmax(hgrn_lb.astype(F32), axis=0), axis=0)[l].reshape(1, KEYW_B)
    return dict(
        ffn1=(ffn1_norm[l], ffn1_w_gate[l].astype(BF16), ffn1_w_up[l].astype(BF16), ffn1_w_down[l].astype(BF16)),
        ffn2=(ffn2_norm[l], ffn2_w_gate[l].astype(BF16), ffn2_w_up[l].astype(BF16), ffn2_w_down[l].astype(BF16)),
        mix_norm=mix_norm[l], w12=w12, bd=bd,
        qg=jnp.tile(q_norm[l], N_HEADS_A).reshape(1, WIDTH_A),
        kg=jnp.tile(k_norm[l], N_HEADS_A).reshape(1, WIDTH_A),
        lb=lb,
        onorm=jnp.tile(hgrn_o_norm[l], N_HEADS_B).reshape(1, WIDTH_B),
        wpa=w_proj_attn[l].astype(BF16), wph=w_proj_hgrn[l].astype(BF16), wo=w_out[l].astype(BF16),
    )


def _pre(x, w, tabs, tm_ffn, tm_in):
    x1 = _ffn(x, *w["ffn1"], tm_ffn)
    names = ("qm", "k", "kb", "v", "vt", "iqm", "ikw", "ik2", "hq", "kf", "lf", "hi", "hg", "ga", "gb")
    outs = _inproj(x1, w["mix_norm"], w["w12"], w["bd"], w["qg"], w["kg"], w["lb"], *tabs, tm_in)
    return x1, dict(zip(names, outs))


def _post(x1, attn, o, p, w, tm_merge, tm_ffn):
    x2 = _merge(attn, o, p["hg"], p["ga"], p["gb"], x1, w["onorm"], w["wpa"], w["wph"], w["wo"], tm_merge)
    return _ffn(x2, *w["ffn2"], tm_ffn)


def _prompt_layer(x_prompt, w):
    b, t, d = x_prompt.shape
    n = b * t
    tabs = _rot_tables(jnp.arange(t, dtype=I32))
    x1, p = _pre(x_prompt.reshape(n, d), w, tabs, min(1024, n), min(512, t))
    r3 = lambda a: a.reshape(b, t, a.shape[-1])
    n_sel = min(TOPK_MAX, t // 4)
    tq = min(ATTN_TILE, t)
    vt = p["vt"].reshape(b, t // tq, WIDTH_A, tq)
    attn = _prompt_attention(r3(p["iqm"]), r3(p["ikw"]), r3(p["ik2"]), r3(p["qm"]), r3(p["kb"]), vt, n_sel, tq)
    s0 = jnp.zeros((b, N_HEADS_B, KEY_DIM_B, VAL_DIM_B), F32)
    o, s_new = _hgrn(r3(p["hq"]), r3(p["kf"]), r3(p["lf"]), r3(p["hi"]), s0, min(256, t))
    y = _post(x1, attn.reshape(n, WIDTH_A), o.reshape(n, WIDTH_B), p, w, min(512, n), min(1024, n))
    return (y.reshape(b, t, d),
            p["k"].reshape(b, t, N_HEADS_A, HEAD_DIM_A), p["v"].reshape(b, t, N_HEADS_A, HEAD_DIM_A),
            p["ikw"][:, :IDX_DIM].reshape(b, t, IDX_DIM), s_new)


IDX_PAGES_PER_STEP = 16
KV_PAGES_PER_STEP = 8


def _sidx_kernel(pt_ref, iq2_ref, w2_ref, iknew_ref, *rest, pps, n_pages, n_sel):
    del pt_ref
    pages = rest[:pps]
    keys_ref, thr_ref = rest[pps:]
    s = pl.program_id(1)
    nblk = keys_ref.shape[1]
    iq2 = iq2_ref[0]
    w2 = w2_ref[0] * IDX_W_SCALE
    tq = iq2.shape[0] // N_HEADS_IDX

    def block_scores(ik_t):
        d = jnp.dot(iq2, ik_t, preferred_element_type=F32)
        r = jnp.maximum(d, 0.0) * w2
        sc = r[0:tq]
        for h in range(1, N_HEADS_IDX):
            sc = sc + r[h * tq:(h + 1) * tq]
        return sc

    for j in range(pps):
        keys_ref[0, s * pps + j] = _float_key(block_scores(pages[j][...].astype(BF16)))

    @pl.when(s == pl.num_programs(1) - 1)
    def _():
        lane = lax.broadcasted_iota(I32, (tq, LANES), 1)
        row = lax.broadcasted_iota(I32, (tq, LANES), 0)
        keys_ref[0, n_pages] = jnp.where(lane <= row, _float_key(block_scores(iknew_ref[0])), INT_MIN)
        for u in range(n_pages + 1, nblk):
            keys_ref[0, u] = jnp.full((tq, LANES), INT_MIN, I32)

        def count_ge(t):
            parts = [jnp.where(keys_ref[0, u] >= t, 1.0, 0.0) for u in range(nblk)]
            return jnp.sum(_tree_sum(parts), axis=1, keepdims=True)

        cand, ccnt = _search_threshold(count_ge, (tq, 1), n_sel)
        thr_ref[0] = jnp.broadcast_to(jnp.maximum(cand, INT_MIN + 1), (tq, LANES))

        @pl.when(jnp.max(ccnt) > n_sel)
        def _():
            need = n_sel - count_ge(cand + 1)
            r2 = lax.broadcasted_iota(I32, (LANES, LANES), 0)
            c2 = lax.broadcasted_iota(I32, (LANES, LANES), 1)
            upper = jnp.where(r2 < c2, 1.0, 0.0)

            def tie_blk(u, seen):
                kk = keys_ref[0, u]
                tf = jnp.where(kk == cand, 1.0, 0.0)
                rank = jnp.dot(tf, upper, preferred_element_type=F32) + seen
                drop = jnp.where(rank >= need, tf, 0.0)
                keys_ref[0, u] = jnp.where(drop > 0.0, INT_MIN, kk)
                return seen + jnp.sum(tf, axis=1, keepdims=True)

            lax.fori_loop(0, n_pages + 1, tie_blk, jnp.zeros((tq, 1), F32))


def _sample_index(page_table, iq2, w2, iknew_t, idx_pages_t, n_sel):
    b, n_pages = page_table.shape
    pps = min(IDX_PAGES_PER_STEP, n_pages)
    tq = iq2.shape[1] // N_HEADS_IDX
    nblk = -(-(n_pages + 1) // SUBLANES) * SUBLANES
    per_b = lambda r, w: pl.BlockSpec((1, r, w), lambda bi, s, pt: (bi, 0, 0))
    page = lambda j: pl.BlockSpec((None, IDX_DIM, PAGE_SIZE), lambda bi, s, pt, j=j: (pt[bi, s * pps + j], 0, 0))
    return pl.pallas_call(
        functools.partial(_sidx_kernel, pps=pps, n_pages=n_pages, n_sel=n_sel),
        grid_spec=pltpu.PrefetchScalarGridSpec(
            num_scalar_prefetch=1,
            grid=(b, n_pages // pps),
            in_specs=[per_b(N_HEADS_IDX * tq, IDX_DIM), per_b(N_HEADS_IDX * tq, LANES), per_b(IDX_DIM, PAGE_SIZE)]
            + [page(j) for j in range(pps)],
            out_specs=[pl.BlockSpec((1, nblk, tq, LANES), lambda bi, s, pt: (bi, 0, 0, 0)),
                       pl.BlockSpec((1, tq, LANES), lambda bi, s, pt: (bi, 0, 0))],
        ),
        out_shape=[jax.ShapeDtypeStruct((b, nblk, tq, LANES), I32), jax.ShapeDtypeStruct((b, tq, LANES), I32)],
        compiler_params=_params("parallel", "arbitrary"),
        name="sidx",
    )(page_table, iq2, w2, iknew_t, *([idx_pages_t] * pps))


def _sattn_kernel(pt_ref, qbd_ref, keys_ref, thr_ref, knew_ref, vnew_ref, *rest, ppb, n_pages):
    del pt_ref
    kpages, vpages = rest[:ppb], rest[ppb:2 * ppb]
    o_ref, m_scr, l_scr, acc_scr = rest[2 * ppb:]
    s = pl.program_id(1)
    nh = N_HEADS_A
    qbd = qbd_ref[0]
    tq = qbd.shape[0] // nh
    thr = jnp.concatenate([thr_ref[0, :, :1]] * nh, axis=0)

    @pl.when(s == 0)
    def _():
        m_scr[...] = jnp.full(m_scr.shape, NEG_BIG, F32)
        l_scr[...] = jnp.zeros(l_scr.shape, F32)
        acc_scr[...] = jnp.zeros(acc_scr.shape, F32)

    def pages_update(kts, vts, blks):
        kt = jnp.concatenate([a.astype(BF16) for a in kts], axis=1)
        vt = jnp.concatenate([a.astype(BF16) for a in vts], axis=1)
        kk = jnp.concatenate([keys_ref[0, blk] for blk in blks], axis=1)
        sc = jnp.dot(qbd, kt, preferred_element_type=F32)
        sc = jnp.where(jnp.concatenate([kk] * nh, axis=0) >= thr, sc, NEG_BIG)
        m_old = m_scr[:, :1]
        m_new = jnp.maximum(m_old, jnp.max(sc, axis=1, keepdims=True))
        alpha = jnp.exp2(m_old - m_new)
        p = jnp.exp2(sc - m_new)
        l_scr[...] = alpha * l_scr[...] + jnp.sum(p, axis=1, keepdims=True)
        pv = lax.dot_general(p.astype(BF16), vt, NT_DIMS, preferred_element_type=F32)
        acc_scr[...] = alpha * acc_scr[...] + pv
        m_scr[...] = jnp.broadcast_to(m_new, m_scr.shape)

    pages_update([r[...] for r in kpages], [r[...] for r in vpages], [s * ppb + j for j in range(ppb)])

    @pl.when(s == pl.num_programs(1) - 1)
    def _():
        pages_update([knew_ref[0]], [vnew_ref[0]], [n_pages])
        acc = acc_scr[...]
        o = jnp.concatenate([acc[h * tq:(h + 1) * tq, h * HEAD_DIM_A:(h + 1) * HEAD_DIM_A] for h in range(nh)],
                            axis=0)
        o_ref[0] = o / l_scr[:, :HEAD_DIM_A]


def _sample_attention(page_table, qbd, keys, thr, knew_t, vnew_t, k_pages_t, v_pages_t):
    b, n_pages = page_table.shape
    ppb = min(KV_PAGES_PER_STEP, n_pages)
    rows = qbd.shape[1]
    per_b3 = lambda r, w: pl.BlockSpec((1, r, w), lambda bi, s, pt: (bi, 0, 0))
    page = lambda j: pl.BlockSpec((None, WIDTH_A, PAGE_SIZE), lambda bi, s, pt, j=j: (pt[bi, s * ppb + j], 0, 0))
    return pl.pallas_call(
        functools.partial(_sattn_kernel, ppb=ppb, n_pages=n_pages),
        grid_spec=pltpu.PrefetchScalarGridSpec(
            num_scalar_prefetch=1,
            grid=(b, n_pages // ppb),
            in_specs=[per_b3(rows, WIDTH_A),
                      pl.BlockSpec((1,) + keys.shape[1:], lambda bi, s, pt: (bi, 0, 0, 0)),
                      per_b3(thr.shape[1], LANES), per_b3(WIDTH_A, PAGE_SIZE), per_b3(WIDTH_A, PAGE_SIZE)]
            + [page(j) for j in range(ppb)] * 2,
            out_specs=per_b3(rows, HEAD_DIM_A),
            scratch_shapes=[pltpu.VMEM((rows, LANES), F32), pltpu.VMEM((rows, LANES), F32),
                            pltpu.VMEM((rows, WIDTH_A), F32)],
        ),
        out_shape=jax.ShapeDtypeStruct((b, rows, HEAD_DIM_A), F32),
        compiler_params=_params("parallel", "arbitrary"),
        name="sattn",
    )(page_table, qbd, keys, thr, knew_t, vnew_t, *([k_pages_t] * ppb), *([v_pages_t] * ppb))


def _unmask_heads(xm, n_heads):
    x = xm.reshape(xm.shape[0], n_heads, LANES)
    return jnp.stack([x[:, h, (h % 2) * HEAD_DIM_A:(h % 2 + 1) * HEAD_DIM_A] for h in range(n_heads)], axis=1)


def _sample_layer(x_sample, cache_k, cache_v, cache_idx_k, state, page_table, w):
    b, t, d = x_sample.shape
    n = b * t
    n_pages = page_table.shape[1]
    past = n_pages * PAGE_SIZE
    pos = past + jnp.arange(t, dtype=I32)
    x1, p = _pre(x_sample.reshape(n, d), w, _rot_tables(jnp.tile(pos, b)), n, n)

    iq2 = _unmask_heads(p["iqm"], N_HEADS_IDX).reshape(b, t, N_HEADS_IDX, IDX_DIM)
    iq2 = iq2.transpose(0, 2, 1, 3).reshape(b, N_HEADS_IDX * t, IDX_DIM)
    w2 = p["ikw"][:, IDX_DIM:IDX_DIM + N_HEADS_IDX].reshape(b, t, N_HEADS_IDX).transpose(0, 2, 1)
    w2 = jnp.broadcast_to(w2.reshape(b, N_HEADS_IDX * t, 1), (b, N_HEADS_IDX * t, LANES))
    new_page = lambda a: jnp.pad(a.reshape(b, t, -1).transpose(0, 2, 1), ((0, 0), (0, 0), (0, PAGE_SIZE - t)))
    n_sel = min(TOPK_MAX, (past + t) // 4)
    keys, thr = _sample_index(page_table, iq2, w2, new_page(p["ik2"][:, :IDX_DIM]),
                              cache_idx_k[0].transpose(0, 2, 1), n_sel)

    q2 = _unmask_heads(p["qm"], N_HEADS_A).reshape(b, t, N_HEADS_A, HEAD_DIM_A).transpose(0, 2, 1, 3)
    eye = jnp.eye(N_HEADS_A, dtype=q2.dtype)
    qbd = (q2[:, :, :, None, :] * eye[None, :, None, :, None]).reshape(b, N_HEADS_A * t, WIDTH_A)
    pages_t = lambda c: c[0].transpose(0, 2, 3, 1).reshape(-1, WIDTH_A, PAGE_SIZE)
    o2 = _sample_attention(page_table, qbd, keys, thr, new_page(p["k"]), new_page(p["v"]),
                           pages_t(cache_k), pages_t(cache_v))
    attn = o2.reshape(b, N_HEADS_A, t, HEAD_DIM_A).transpose(0, 2, 1, 3).reshape(n, WIDTH_A)

    tp = -(-t // CHUNK_B) * CHUNK_B
    r3 = lambda a: jnp.pad(a.reshape(b, t, a.shape[-1]), ((0, 0), (0, tp - t), (0, 0)))
    o, s_new = _hgrn(r3(p["hq"]), r3(p["kf"]), r3(p["lf"]), r3(p["hi"]), state, tp)
    y = _post(x1, attn, o[:, :t].reshape(n, WIDTH_B), p, w, n, n)
    return (y.reshape(b, t, d),
            p["k"].reshape(b, t, N_HEADS_A, HEAD_DIM_A), p["v"].reshape(b, t, N_HEADS_A, HEAD_DIM_A),
            p["ikw"][:, :IDX_DIM].reshape(b, t, IDX_DIM), s_new)


def kernel(x_prompt, x_sample, cache_k, cache_v, cache_idx_k, state_hgrn, page_table,
           ffn1_norm, ffn1_w_gate, ffn1_w_up, ffn1_w_down, mix_norm, w_in, q_norm, k_norm,
           hgrn_lb, hgrn_o_norm, w_proj_attn, w_proj_hgrn, w_out,
           ffn2_norm, ffn2_w_gate, ffn2_w_up, ffn2_w_down):
    w = _prep_weights(ffn1_norm, ffn1_w_gate, ffn1_w_up, ffn1_w_down, mix_norm, w_in, q_norm, k_norm,
                      hgrn_lb, hgrn_o_norm, w_proj_attn, w_proj_hgrn, w_out,
                      ffn2_norm, ffn2_w_gate, ffn2_w_up, ffn2_w_down)
    yp, kp, vp, ikp, sp = _prompt_layer(x_prompt, w)
    ys, ks, vs, iks, ss = _sample_layer(x_sample, cache_k, cache_v, cache_idx_k, state_hgrn[0], page_table, w)
    return (yp, ys, kp[None], vp[None], ikp[None], sp[None], ks[None], vs[None], iks[None], ss[None])
```

```python
import functools

import numpy as np
import jax
import jax.numpy as jnp
from jax import lax
from jax.experimental import pallas as pl
from jax.experimental.pallas import tpu as pltpu

F32 = jnp.float32
BF16 = jnp.bfloat16
I32 = jnp.int32

D_MODEL = 1024
PAST_LEN = 16384
PAGE_SIZE = 128
N_HEADS_A = 8
HEAD_DIM_A = 64
ROT_DIM = HEAD_DIM_A // 4
ROPE_THETA = 500000.0
N_HEADS_IDX = 4
IDX_DIM = 64
IDX_W_SCALE = (N_HEADS_IDX * IDX_DIM) ** -0.5
TOPK_MAX = 256
N_HEADS_B = 4
KEY_DIM_B = 128
VAL_DIM_B = 128
CHUNK_B = 16
WIDTH_A = N_HEADS_A * HEAD_DIM_A
WIDTH_B = N_HEADS_B * VAL_DIM_B
KEYW_B = N_HEADS_B * KEY_DIM_B
EPS = 1e-6
IN_SIZES = (WIDTH_A, WIDTH_A, WIDTH_A, N_HEADS_IDX * IDX_DIM, IDX_DIM, N_HEADS_IDX,
            KEYW_B, KEYW_B, WIDTH_B, WIDTH_B, D_MODEL, D_MODEL)
IN_OFFSETS = tuple(int(o) for o in np.cumsum(IN_SIZES)[:-1])

LANES = 128
SUBLANES = 8
VMEM_LIMIT_BYTES = 56 * 1024 * 1024
INT_MIN = np.int32(-2 ** 31)
NEG_BIG = -1e30
FF_CHUNK = 256
IN_CHUNK = 512
NT_DIMS = (((1,), (1,)), ((), ()))
LOG2E = 1.4426950408889634
ATTN_TILE = 256


def _params(*sem):
    return pltpu.CompilerParams(dimension_semantics=sem, vmem_limit_bytes=VMEM_LIMIT_BYTES)


def _float_key(x):
    u = pltpu.bitcast(x + 0.0, I32)
    return u ^ ((u >> 31) & np.int32(0x7FFFFFFF))


def _ffn_kernel(x_ref, g_ref, wg_ref, wu_ref, wd_ref, o_ref, h_scr, acc_scr):
    j = pl.program_id(1)

    @pl.when(j == 0)
    def _():
        x = x_ref[...]
        ms = jnp.mean(x * x, axis=-1, keepdims=True)
        h_scr[...] = (x * lax.rsqrt(ms + EPS) * g_ref[...]).astype(BF16)
        acc_scr[...] = jnp.zeros_like(acc_scr)

    h = h_scr[...]
    g = jnp.dot(h, wg_ref[...], preferred_element_type=F32)
    u = jnp.dot(h, wu_ref[...], preferred_element_type=F32)
    a = (g * jax.nn.sigmoid(g) * u).astype(BF16)
    acc_scr[...] += jnp.dot(a, wd_ref[...], preferred_element_type=F32)

    @pl.when(j == pl.num_programs(1) - 1)
    def _():
        o_ref[...] = x_ref[...] + 0.5 * acc_scr[...]


def _ffn(x, gain, wg, wu, wd, tm):
    n, d = x.shape
    ff = wg.shape[1]
    return pl.pallas_call(
        _ffn_kernel,
        grid=(n // tm, ff // FF_CHUNK),
        in_specs=[
            pl.BlockSpec((tm, d), lambda i, j: (i, 0)),
            pl.BlockSpec((1, d), lambda i, j: (0, 0)),
            pl.BlockSpec((d, FF_CHUNK), lambda i, j: (0, j)),
            pl.BlockSpec((d, FF_CHUNK), lambda i, j: (0, j)),
            pl.BlockSpec((FF_CHUNK, d), lambda i, j: (j, 0)),
        ],
        out_specs=pl.BlockSpec((tm, d), lambda i, j: (i, 0)),
        out_shape=jax.ShapeDtypeStruct((n, d), F32),
        scratch_shapes=[pltpu.VMEM((tm, d), BF16), pltpu.VMEM((tm, d), F32)],
        compiler_params=_params("parallel", "arbitrary"),
        name="ffn",
    )(x, gain.reshape(1, d), wg, wu, wd)


_J_Q, _J_K, _J_V, _J_IDX, _J_HQ, _J_HF, _J_HI, _J_HG, _J_GA0, _J_GA1, _J_GB0, _J_GB1 = range(12)


def _rot(xc, c, sa, sb):
    return (xc * c + pltpu.roll(xc, LANES - ROT_DIM // 2, 1) * sa
            + pltpu.roll(xc, ROT_DIM // 2, 1) * sb)


def _inproj_kernel(x_ref, g_ref, w_ref, bd_ref, qg_ref, kg_ref, lb_ref, cos_ref, sa_ref, sb_ref,
                   qm_ref, k_ref, kb_ref, v_ref, vt_ref, iqm_ref, ikw_ref, ik2_ref,
                   hq_ref, kf_ref, lf_ref, hi_ref, hg_ref, ga_ref, gb_ref, h_scr, *, kv_cols):
    j = pl.program_id(1)
    tm = x_ref.shape[0]

    @pl.when(j == 0)
    def _():
        x = x_ref[...]
        ms = jnp.mean(x * x, axis=-1, keepdims=True)
        h_scr[...] = (x * lax.rsqrt(ms + EPS) * g_ref[...]).astype(BF16)

    y = jnp.dot(h_scr[...], w_ref[0], preferred_element_type=F32)
    lane = lax.broadcasted_iota(I32, (tm, LANES), 1)
    lo_half = lane < HEAD_DIM_A

    def head_norm(t, gain):
        ms = jnp.dot((t * t).astype(BF16), bd_ref[...], preferred_element_type=F32)
        return t * lax.rsqrt(ms + EPS) * gain

    @pl.when(j == _J_Q)
    def _():
        yn = head_norm(y, qg_ref[...])
        c, sa, sb = cos_ref[...], sa_ref[...], sb_ref[...]
        for p in range(WIDTH_A // LANES):
            r = _rot(yn[:, p * LANES:(p + 1) * LANES], c, sa, sb) * (HEAD_DIM_A ** -0.5 * LOG2E)
            qm_ref[:, (2 * p) * LANES:(2 * p + 1) * LANES] = jnp.where(lo_half, r, 0.0).astype(BF16)
            qm_ref[:, (2 * p + 1) * LANES:(2 * p + 2) * LANES] = jnp.where(lo_half, 0.0, r).astype(BF16)

    @pl.when(j == _J_K)
    def _():
        yn = head_norm(y, kg_ref[...])
        c, sa, sb = cos_ref[...], sa_ref[...], sb_ref[...]
        for p in range(WIDTH_A // LANES):
            sl = slice(p * LANES, (p + 1) * LANES)
            r = _rot(yn[:, sl], c, sa, sb)
            if kv_cols:
                k_ref[0, sl, :] = r.T
            else:
                k_ref[:, sl] = r
            kb_ref[:, sl] = r.astype(BF16)

    @pl.when(j == _J_V)
    def _():
        yt = y.T
        if kv_cols:
            v_ref[0] = yt
        else:
            v_ref[...] = y
        for u in range(vt_ref.shape[0]):
            vt_ref[u] = yt[:, u * vt_ref.shape[2]:(u + 1) * vt_ref.shape[2]].astype(BF16)

    @pl.when(j == _J_IDX)
    def _():
        c, sa, sb = cos_ref[...], sa_ref[...], sb_ref[...]
        for p in range(2):
            r = _rot(y[:, p * LANES:(p + 1) * LANES], c, sa, sb)
            iqm_ref[:, (2 * p) * LANES:(2 * p + 1) * LANES] = jnp.where(lo_half, r, 0.0).astype(BF16)
            iqm_ref[:, (2 * p + 1) * LANES:(2 * p + 2) * LANES] = jnp.where(lo_half, 0.0, r).astype(BF16)
        r = _rot(y[:, 2 * LANES:3 * LANES], jnp.where(lo_half, c, 1.0),
                 jnp.where(lo_half, sa, 0.0), jnp.where(lo_half, sb, 0.0))
        if kv_cols:
            ikw_ref[0] = r.T
        else:
            ikw_ref[...] = r
        ik2_ref[...] = jnp.where(lo_half, r, pltpu.roll(r, HEAD_DIM_A, 1)).astype(BF16)

    @pl.when(j == _J_HQ)
    def _():
        hq_ref[...] = (y * jax.nn.sigmoid(y)).astype(BF16)

    @pl.when(j == _J_HF)
    def _():
        lb = lb_ref[...]
        lf_ref[...] = jnp.log(lb + (1.0 - lb) * jax.nn.sigmoid(y))
        kf_ref[...] = ((1.0 - lb) * jax.nn.sigmoid(-y)).astype(BF16)

    @pl.when(j == _J_HI)
    def _():
        hi_ref[...] = y.astype(BF16)

    @pl.when(j == _J_HG)
    def _():
        hg_ref[...] = (y * jax.nn.sigmoid(y)).astype(BF16)

    for jj, ref, half in ((_J_GA0, ga_ref, 0), (_J_GA1, ga_ref, 1), (_J_GB0, gb_ref, 0), (_J_GB1, gb_ref, 1)):
        @pl.when(j == jj)
        def _(ref=ref, half=half):
            ref[:, half * IN_CHUNK:(half + 1) * IN_CHUNK] = jax.nn.sigmoid(y).astype(BF16)


def _inproj(x, gain, w12, bd, qg, kg, lb, cos_t, sa_t, sb_t, tm, kv_cols):
    n, d = x.shape
    nt = cos_t.shape[0] // tm
    row = lambda w: pl.BlockSpec((tm, w), lambda i, j: (i, 0))
    const = lambda r, w: pl.BlockSpec((r, w), lambda i, j: (0, 0))
    tab = pl.BlockSpec((tm, LANES), lambda i, j: (i % nt, 0))
    vt_tile = min(ATTN_TILE, tm)

    def rows(w, dt):
        return row(w), jax.ShapeDtypeStruct((n, w), dt)

    def kv(w):
        if kv_cols:
            return (pl.BlockSpec((1, w, tm), lambda i, j: (i // nt, 0, i % nt)),
                    jax.ShapeDtypeStruct((n // (nt * tm), w, nt * tm), F32))
        return rows(w, F32)

    outs = [rows(2 * WIDTH_A, BF16), kv(WIDTH_A), rows(WIDTH_A, BF16), kv(WIDTH_A),
            (pl.BlockSpec((tm // vt_tile, WIDTH_A, vt_tile), lambda i, j: (i, 0, 0)),
             jax.ShapeDtypeStruct((n // vt_tile, WIDTH_A, vt_tile), BF16)),
            rows(2 * N_HEADS_IDX * IDX_DIM, BF16), kv(LANES), rows(LANES, BF16),
            rows(KEYW_B, BF16), rows(KEYW_B, BF16), rows(KEYW_B, F32), rows(WIDTH_B, BF16), rows(WIDTH_B, BF16),
            rows(D_MODEL, BF16), rows(D_MODEL, BF16)]
    return pl.pallas_call(
        functools.partial(_inproj_kernel, kv_cols=kv_cols),
        grid=(n // tm, w12.shape[0]),
        in_specs=[
            row(d), const(1, d),
            pl.BlockSpec((1, d, IN_CHUNK), lambda i, j: (j, 0, 0)),
            const(WIDTH_A, WIDTH_A), const(1, WIDTH_A), const(1, WIDTH_A), const(1, KEYW_B),
            tab, tab, tab,
        ],
        out_specs=[o[0] for o in outs],
        out_shape=[o[1] for o in outs],
        scratch_shapes=[pltpu.VMEM((tm, d), BF16)],
        compiler_params=_params("parallel", "arbitrary"),
        name="inproj",
    )(x, gain.reshape(1, d), w12, bd, qg, kg, lb, cos_t, sa_t, sb_t)


def _tree(parts, op):
    while len(parts) > 1:
        parts = [op(a, b) for a, b in zip(parts[0::2], parts[1::2])] + ([parts[-1]] if len(parts) % 2 else [])
    return parts[0]


def _tree_sum(parts):
    return _tree(parts, jnp.add)


def _refine_threshold(cand, n_sel, count_fn, next_above, next_below):
    neg_flt_max_key = np.int32(-2139095040)
    key = jnp.maximum(cand, neg_flt_max_key)
    t = pltpu.bitcast(key ^ ((key >> 31) & np.int32(0x7FFFFFFF)), F32)

    def state(t):
        cge, cgt = count_fn(t, False), count_fn(t, True)
        lower = jnp.where(cand == INT_MIN, 0.0, -1.0)
        return cge, cgt, jnp.where(cgt >= n_sel, 1.0, jnp.where(cge < n_sel, lower, 0.0))

    def body(st):
        t, _, _, move = st
        t = jnp.where(move > 0.0, next_above(t), jnp.where(move < 0.0, next_below(t), t))
        return (t,) + state(t)

    t, cge, cgt, _ = lax.while_loop(lambda st: jnp.max(jnp.abs(st[3])) > 0.0, body, (t,) + state(t))
    return t, cge, cgt


def _search_threshold(count_ge, shape, n_sel):
    c0 = count_ge(jnp.zeros(shape, I32))
    ok = c0 >= n_sel
    cand = jnp.where(ok, np.int32(0), INT_MIN)
    ccnt = jnp.where(ok, c0, 0.0)

    def bit_body(i, carry):
        cand, ccnt = carry
        t = cand + (jnp.int32(1) << (30 - i))
        c = count_ge(t)
        ok = c >= n_sel
        return jnp.where(ok, t, cand), jnp.where(ok, c, ccnt)

    return lax.fori_loop(0, 31, bit_body, (cand, ccnt))


def _pattn_kernel(iqm_ref, ikw_ref, ik2_ref, qm_ref, k_ref, vt_ref, o_ref,
                  keys_scr, sc_scr, m_scr, a_scr, acc_scr, s_scr, p_scr, *, n_sel, tq):
    qi = pl.program_id(1)
    nkb = qi + 1
    krow = lax.broadcasted_iota(I32, (tq, tq), 0)
    qcol = lax.broadcasted_iota(I32, (tq, tq), 1)
    half = HEAD_DIM_A

    ikw_t = ikw_ref[0]
    wrow = [ikw_t[IDX_DIM + h:IDX_DIM + h + 1, :] * IDX_W_SCALE for h in range(N_HEADS_IDX)]

    def score_blk(kb, c):
        ik2 = ik2_ref[0, pl.ds(pl.multiple_of(kb * tq, tq), tq), :]
        sc = jnp.zeros((tq, tq), F32)
        for h in range(N_HEADS_IDX):
            d = lax.dot_general(ik2, iqm_ref[0, :, h * LANES:(h + 1) * LANES], NT_DIMS,
                                preferred_element_type=F32)
            sc = sc + jnp.maximum(d, 0.0) * wrow[h]
        off = jnp.where(kb < qi, jnp.int32(tq), jnp.int32(0))
        adm = krow <= qcol + off
        keys_scr[kb] = jnp.where(adm, _float_key(sc), INT_MIN)
        sc_scr[kb] = jnp.where(adm, sc, -jnp.inf)
        return c

    lax.fori_loop(0, nkb, score_blk, 0)

    def over_rows(tile_fn, op, init):
        def body(kb, acc):
            parts = [tile_fn(kb, slice(r * SUBLANES, (r + 1) * SUBLANES)) for r in range(tq // SUBLANES)]
            return op(acc, _tree(parts, op))
        return lax.fori_loop(0, nkb, body, jnp.full((SUBLANES, tq), init, F32))

    def count_ge(t):
        acc = over_rows(lambda kb, rs: jnp.where(keys_scr[kb, rs, :] >= t, 1.0, 0.0), jnp.add, 0.0)
        return jnp.sum(acc, axis=0, keepdims=True)

    cand, _ = _search_threshold(count_ge, (1, tq), n_sel)

    def fcount(t, strict):
        hit = (lambda s: s > t) if strict else (lambda s: s >= t)
        acc = over_rows(lambda kb, rs: jnp.where(hit(sc_scr[kb, rs, :]), 1.0, 0.0), jnp.add, 0.0)
        return jnp.sum(acc, axis=0, keepdims=True)

    def next_above(t):
        def tile(kb, rs):
            s = sc_scr[kb, rs, :]
            return jnp.where(s > t, s, jnp.inf)
        return jnp.min(over_rows(tile, jnp.minimum, jnp.inf), axis=0, keepdims=True)

    def next_below(t):
        def tile(kb, rs):
            s = sc_scr[kb, rs, :]
            return jnp.where(s < t, s, -jnp.inf)
        return jnp.max(over_rows(tile, jnp.maximum, -jnp.inf), axis=0, keepdims=True)

    thr, cge, cgt = _refine_threshold(cand, n_sel, fcount, next_above, next_below)

    @pl.when(jnp.max(cge) > n_sel)
    def _():
        need = n_sel - cgt
        lower = jnp.where(qcol < krow, 1.0, 0.0).astype(BF16)

        def tie_blk(kb, seen):
            s = sc_scr[kb]
            tf = jnp.where(s == thr, 1.0, 0.0)
            rank = jnp.dot(lower, tf.astype(BF16), preferred_element_type=F32) + seen
            drop = jnp.where(rank >= need, tf, 0.0)
            sc_scr[kb] = jnp.where(drop > 0.0, -jnp.inf, s)
            return seen + jnp.sum(tf, axis=0, keepdims=True)

        lax.fori_loop(0, nkb, tie_blk, jnp.zeros((1, tq), F32))

    m_scr[...] = jnp.full(m_scr.shape, NEG_BIG, F32)
    acc_scr[...] = jnp.zeros(acc_scr.shape, F32)
    ones_half = jnp.ones((half, tq), BF16)

    def attn_blk(kb, c):
        bias = jnp.where(sc_scr[kb] >= thr, 0.0, NEG_BIG)
        ks = pl.ds(pl.multiple_of(kb * tq, tq), tq)
        for h in range(N_HEADS_A):
            pr = h // 2
            s_scr[h] = lax.dot_general(k_ref[0, ks, pr * LANES:(pr + 1) * LANES],
                                       qm_ref[0, :, h * LANES:(h + 1) * LANES], NT_DIMS,
                                       preferred_element_type=F32) + bias
        for h in range(N_HEADS_A):
            s = s_scr[h]
            m_old = m_scr[h]
            m_new = jnp.maximum(m_old, jnp.max(s, axis=0, keepdims=True))
            a_scr[h] = jnp.exp2(m_old - m_new)
            p_scr[h] = jnp.exp2(s - m_new).astype(BF16)
            m_scr[h] = m_new
        for h in range(N_HEADS_A):
            pr = h // 2
            vt = vt_ref[0, kb, pr * LANES:(pr + 1) * LANES, :]
            lhs = (jnp.concatenate([vt[:half], ones_half], axis=0) if h % 2 == 0
                   else jnp.concatenate([ones_half, vt[half:]], axis=0))
            acc_scr[h] = a_scr[h] * acc_scr[h] + jnp.dot(lhs, p_scr[h], preferred_element_type=F32)
        return c

    lax.fori_loop(0, nkb, attn_blk, 0)
    for pr in range(N_HEADS_A // 2):
        ae, ao = acc_scr[2 * pr], acc_scr[2 * pr + 1]
        ot = jnp.concatenate([ae[:half] / ae[half:], ao[half:] / ao[:half]], axis=0)
        o_ref[0, :, pr * LANES:(pr + 1) * LANES] = ot.T


def _prompt_attention(iqm, ikw, ik2, qm, kb, vt, n_sel, tq):
    b, t, _ = qm.shape
    nq = t // tq
    qblk = lambda w: pl.BlockSpec((1, tq, w), lambda bi, qi: (bi, qi, 0))
    full = lambda w: pl.BlockSpec((1, t, w), lambda bi, qi: (bi, 0, 0))
    return pl.pallas_call(
        functools.partial(_pattn_kernel, n_sel=n_sel, tq=tq),
        grid=(b, nq),
        in_specs=[qblk(iqm.shape[2]), pl.BlockSpec((1, LANES, tq), lambda bi, qi: (bi, 0, qi)), full(LANES),
                  qblk(qm.shape[2]), full(WIDTH_A),
                  pl.BlockSpec((1, nq, WIDTH_A, tq), lambda bi, qi: (bi, 0, 0, 0))],
        out_specs=qblk(WIDTH_A),
        out_shape=jax.ShapeDtypeStruct((b, t, WIDTH_A), F32),
        scratch_shapes=[
            pltpu.VMEM((nq, tq, tq), I32),
            pltpu.VMEM((nq, tq, tq), F32),
            pltpu.VMEM((N_HEADS_A, 1, tq), F32),
            pltpu.VMEM((N_HEADS_A, 1, tq), F32),
            pltpu.VMEM((N_HEADS_A, LANES, tq), F32),
            pltpu.VMEM((N_HEADS_A, tq, tq), F32),
            pltpu.VMEM((N_HEADS_A, tq, tq), BF16),
        ],
        compiler_params=_params("parallel", "arbitrary"),
        name="pattn",
    )(iqm, ikw, ik2, qm, kb, vt)


def _hgrn_kernel(q_ref, k_ref, g_ref, v_ref, s0_ref, o_ref, sn_ref, s_scr, *, tc):
    ti = pl.program_id(1)
    c_ = CHUNK_B

    streams = [(bi, h) for bi in range(q_ref.shape[0]) for h in range(N_HEADS_B)]

    @pl.when(ti == 0)
    def _():
        for si, (bi, h) in enumerate(streams):
            s_scr[si] = s0_ref[bi, h].T

    rowi = lax.broadcasted_iota(I32, (c_, LANES), 0)
    lanei = lax.broadcasted_iota(I32, (c_, LANES), 1)

    def chunk(ci, carry):
        rs = pl.ds(pl.multiple_of(ci * c_, c_), c_)
        for si, (bi, h) in enumerate(streams):
            hs = slice(h * LANES, (h + 1) * LANES)
            q = q_ref[bi, rs, hs].astype(F32)
            k = k_ref[bi, rs, hs].astype(F32)
            v = v_ref[bi, rs, hs]
            b = g_ref[bi, rs, hs] * LOG2E
            for sh in (1, 2, 4, 8):
                b = b + jnp.where(rowi >= sh, pltpu.roll(b, sh, 0), 0.0)
            b_last = b[c_ - 1:c_, :]
            st_old = s_scr[si]
            o = lax.dot_general((q * jnp.exp2(b)).astype(BF16), st_old.astype(BF16), NT_DIMS,
                                preferred_element_type=F32)
            cols = []
            for t in range(c_):
                e = jnp.exp2(jnp.minimum(b[t:t + 1, :] - b, 0.0))
                col = jnp.sum(e * k * q[t:t + 1, :], axis=1, keepdims=True)
                cols.append(jnp.where(lanei == t, col, 0.0))
            att = jnp.where(rowi <= lanei, _tree_sum(cols), 0.0).T[:c_]
            o_ref[bi, rs, hs] = o + jnp.dot(att.astype(BF16), v, preferred_element_type=F32)
            kd = k * jnp.exp2(b_last - b)
            upd = jnp.dot(v.astype(F32).T.astype(BF16), kd.astype(BF16), preferred_element_type=F32)
            s_scr[si] = st_old * jnp.exp2(b_last) + upd
        return carry

    lax.fori_loop(0, tc // c_, chunk, 0)

    @pl.when(ti == pl.num_programs(1) - 1)
    def _():
        for si, (bi, h) in enumerate(streams):
            sn_ref[bi, h] = s_scr[si].T


HGRN_BATCH_GROUP = 4


def _hgrn(q, k, g, v, s0, tc):
    b, t, w = q.shape
    bg = min(HGRN_BATCH_GROUP, b)
    blk = pl.BlockSpec((bg, tc, w), lambda bi, ti: (bi, ti, 0))
    st = pl.BlockSpec((bg, N_HEADS_B, KEY_DIM_B, VAL_DIM_B), lambda bi, ti: (bi, 0, 0, 0))
    return pl.pallas_call(
        functools.partial(_hgrn_kernel, tc=tc),
        grid=(b // bg, t // tc),
        in_specs=[blk, blk, blk, blk, st],
        out_specs=[blk, st],
        out_shape=[jax.ShapeDtypeStruct((b, t, w), F32), jax.ShapeDtypeStruct(s0.shape, F32)],
        scratch_shapes=[pltpu.VMEM((bg * N_HEADS_B, KEY_DIM_B, VAL_DIM_B), F32)],
        compiler_params=_params("parallel", "arbitrary"),
        name="hgrn",
    )(q, k, g, v, s0)


def _merge_kernel(attn_ref, o_ref, hg_ref, ga_ref, gb_ref, x_ref, on_ref, wpa_ref, wph_ref, wo_ref, y_ref):
    o = o_ref[...]
    parts = []
    for h in range(N_HEADS_B):
        oh = o[:, h * LANES:(h + 1) * LANES]
        ms = jnp.mean(oh * oh, axis=-1, keepdims=True)
        parts.append(oh * lax.rsqrt(ms + EPS))
    on = jnp.concatenate(parts, axis=1) * on_ref[...] * hg_ref[...]
    pa = jnp.dot(attn_ref[...].astype(BF16), wpa_ref[...], preferred_element_type=F32)
    ph = jnp.dot(on.astype(BF16), wph_ref[...], preferred_element_type=F32)
    merged = ga_ref[...] * pa + gb_ref[...] * ph
    y_ref[...] = x_ref[...] + jnp.dot(merged.astype(BF16), wo_ref[...], preferred_element_type=F32)


def _merge(attn, o, hg, ga, gb, x, onorm, wpa, wph, wo, tm):
    n, d = x.shape
    row = lambda w: pl.BlockSpec((tm, w), lambda i: (i, 0))
    const = lambda r, w: pl.BlockSpec((r, w), lambda i: (0, 0))
    return pl.pallas_call(
        _merge_kernel,
        grid=(n // tm,),
        in_specs=[row(WIDTH_A), row(WIDTH_B), row(WIDTH_B), row(d), row(d), row(d),
                  const(1, WIDTH_B), const(WIDTH_A, d), const(WIDTH_B, d), const(d, d)],
        out_specs=row(d),
        out_shape=jax.ShapeDtypeStruct((n, d), F32),
        compiler_params=_params("parallel"),
        name="merge",
    )(attn, o, hg, ga, gb, x, onorm, wpa, wph, wo)


def _rot_tables(pos):
    r = pos.shape[0]
    inv = ROPE_THETA ** (-jnp.arange(0, ROT_DIM, 2, dtype=F32) / ROT_DIM)
    ang = pos.astype(F32)[:, None] * inv[None, :]
    cos, sin = jnp.cos(ang), jnp.sin(ang)
    half = ROT_DIM // 2
    rest = HEAD_DIM_A - ROT_DIM
    one, z8, zr = jnp.ones((r, rest), F32), jnp.zeros((r, half), F32), jnp.zeros((r, rest), F32)
    c64 = jnp.concatenate([cos, cos, one], axis=1)
    a64 = jnp.concatenate([-sin, z8, zr], axis=1)
    b64 = jnp.concatenate([z8, sin, zr], axis=1)
    return tuple(jnp.concatenate([t, t], axis=1) for t in (c64, a64, b64))


def _prep_weights(ffn1_norm, ffn1_w_gate, ffn1_w_up, ffn1_w_down, mix_norm, w_in, q_norm, k_norm,
                  hgrn_lb, hgrn_o_norm, w_proj_attn, w_proj_hgrn, w_out,
                  ffn2_norm, ffn2_w_gate, ffn2_w_up, ffn2_w_down):
    l = 0
    q, k, v, iq, ik, iw, hq, hf, hi, hg, ga, gb = jnp.split(w_in[l], IN_OFFSETS, axis=1)
    pad = jnp.zeros((D_MODEL, IN_CHUNK - iq.shape[1] - ik.shape[1] - iw.shape[1]), F32)
    idx = jnp.concatenate([iq, ik, iw, pad], axis=1)
    w12 = jnp.stack([q, k, v, idx, hq, hf, hi, hg,
                     ga[:, :IN_CHUNK], ga[:, IN_CHUNK:], gb[:, :IN_CHUNK], gb[:, IN_CHUNK:]]).astype(BF16)
    head = np.arange(WIDTH_A) // HEAD_DIM_A
    bd = jnp.asarray((head[:, None] == head[None, :]).astype(np.float32) / HEAD_DIM_A, BF16)
    lb = jnp.cumsum(jax.nn.softmax(hgrn_lb.astype(F32), axis=0), axis=0)[l].reshape(1, KEYW_B)
    return dict(
        ffn1=(ffn1_norm[l], ffn1_w_gate[l].astype(BF16), ffn1_w_up[l].astype(BF16), ffn1_w_down[l].astype(BF16)),
        ffn2=(ffn2_norm[l], ffn2_w_gate[l].astype(BF16), ffn2_w_up[l].astype(BF16), ffn2_w_down[l].astype(BF16)),
        mix_norm=mix_norm[l], w12=w12, bd=bd,
        qg=jnp.tile(q_norm[l], N_HEADS_A).reshape(1, WIDTH_A),
        kg=jnp.tile(k_norm[l], N_HEADS_A).reshape(1, WIDTH_A),
        lb=lb,
        onorm=jnp.tile(hgrn_o_norm[l], N_HEADS_B).reshape(1, WIDTH_B),
        wpa=w_proj_attn[l].astype(BF16), wph=w_proj_hgrn[l].astype(BF16), wo=w_out[l].astype(BF16),
    )


def _pre(x, w, tabs, tm_ffn, tm_in, kv_cols):
    x1 = _ffn(x, *w["ffn1"], tm_ffn)
    names = ("qm", "k", "kb", "v", "vt", "iqm", "ikw", "ik2", "hq", "kf", "lf", "hi", "hg", "ga", "gb")
    outs = _inproj(x1, w["mix_norm"], w["w12"], w["bd"], w["qg"], w["kg"], w["lb"], *tabs, tm_in, kv_cols)
    return x1, dict(zip(names, outs))


def _post(x1, attn, o, p, w, tm_merge, tm_ffn):
    x2 = _merge(attn, o, p["hg"], p["ga"], p["gb"], x1, w["onorm"], w["wpa"], w["wph"], w["wo"], tm_merge)
    return _ffn(x2, *w["ffn2"], tm_ffn)


def _prompt_layer(x_prompt, w):
    b, t, d = x_prompt.shape
    n = b * t
    tabs = _rot_tables(jnp.arange(t, dtype=I32))
    x1, p = _pre(x_prompt.reshape(n, d), w, tabs, min(1024, n), min(512, t), True)
    r3 = lambda a: a.reshape(b, t, a.shape[-1])
    n_sel = min(TOPK_MAX, t // 4)
    tq = min(ATTN_TILE, t)
    vt = p["vt"].reshape(b, t // tq, WIDTH_A, tq)
    attn = _prompt_attention(r3(p["iqm"]), p["ikw"], r3(p["ik2"]), r3(p["qm"]), r3(p["kb"]), vt, n_sel, tq)
    s0 = jnp.zeros((b, N_HEADS_B, KEY_DIM_B, VAL_DIM_B), F32)
    o, s_new = _hgrn(r3(p["hq"]), r3(p["kf"]), r3(p["lf"]), r3(p["hi"]), s0, min(256, t))
    y = _post(x1, attn.reshape(n, WIDTH_A), o.reshape(n, WIDTH_B), p, w, min(512, n), min(1024, n))
    heads = lambda a: a.reshape(b, N_HEADS_A, HEAD_DIM_A, t).transpose(0, 3, 1, 2)
    return (y.reshape(b, t, d), heads(p["k"]), heads(p["v"]),
            p["ikw"][:, :IDX_DIM, :].transpose(0, 2, 1), s_new)


IDX_PAGES_PER_STEP = 128
KV_PAGES_PER_STEP = 16


def _sidx_kernel(pt_ref, iq2_ref, w2_ref, iknew_ref, *rest, pps, n_pages, n_sel):
    del pt_ref
    pages = rest[:pps]
    sc_ref, thr_ref, keys_scr = rest[pps:]
    s = pl.program_id(1)
    nblk = sc_ref.shape[1]
    iq2 = iq2_ref[0]
    w2 = w2_ref[0] * IDX_W_SCALE
    tq = iq2.shape[0] // N_HEADS_IDX

    def block_scores(ik_t):
        d = jnp.dot(iq2, ik_t, preferred_element_type=F32)
        r = jnp.maximum(d, 0.0) * w2
        sc = r[0:tq]
        for h in range(1, N_HEADS_IDX):
            sc = sc + r[h * tq:(h + 1) * tq]
        return sc

    for j in range(pps):
        sc = block_scores(pages[j][...].astype(BF16))
        sc_ref[0, s * pps + j] = sc
        keys_scr[s * pps + j] = _float_key(sc)

    @pl.when(s == pl.num_programs(1) - 1)
    def _():
        lane = lax.broadcasted_iota(I32, (tq, LANES), 1)
        row = lax.broadcasted_iota(I32, (tq, LANES), 0)
        sc = block_scores(iknew_ref[0])
        sc_ref[0, n_pages] = jnp.where(lane <= row, sc, -jnp.inf)
        keys_scr[n_pages] = jnp.where(lane <= row, _float_key(sc), INT_MIN)
        for u in range(n_pages + 1, nblk):
            sc_ref[0, u] = jnp.full((tq, LANES), -jnp.inf, F32)
            keys_scr[u] = jnp.full((tq, LANES), INT_MIN, I32)

        def over_blocks(tile_fn, op):
            return _tree([tile_fn(u) for u in range(nblk)], op)

        def count_ge(t):
            return jnp.sum(over_blocks(lambda u: jnp.where(keys_scr[u] >= t, 1.0, 0.0), jnp.add),
                           axis=1, keepdims=True)

        def fcount(t, strict):
            hit = (lambda x: x > t) if strict else (lambda x: x >= t)
            return jnp.sum(over_blocks(lambda u: jnp.where(hit(sc_ref[0, u]), 1.0, 0.0), jnp.add),
                           axis=1, keepdims=True)

        def next_above(t):
            def tile(u):
                x = sc_ref[0, u]
                return jnp.where(x > t, x, jnp.inf)
            return jnp.min(over_blocks(tile, jnp.minimum), axis=1, keepdims=True)

        def next_below(t):
            def tile(u):
                x = sc_ref[0, u]
                return jnp.where(x < t, x, -jnp.inf)
            return jnp.max(over_blocks(tile, jnp.maximum), axis=1, keepdims=True)

        cand, _ = _search_threshold(count_ge, (tq, 1), n_sel)
        thr, cge, cgt = _refine_threshold(cand, n_sel, fcount, next_above, next_below)
        thr_ref[0] = jnp.broadcast_to(thr, (tq, LANES))

        @pl.when(jnp.max(cge) > n_sel)
        def _():
            need = n_sel - cgt
            r2 = lax.broadcasted_iota(I32, (LANES, LANES), 0)
            c2 = lax.broadcasted_iota(I32, (LANES, LANES), 1)
            upper = jnp.where(r2 < c2, 1.0, 0.0)

            def tie_blk(u, seen):
                x = sc_ref[0, u]
                tf = jnp.where(x == thr, 1.0, 0.0)
                rank = jnp.dot(tf, upper, preferred_element_type=F32) + seen
                drop = jnp.where(rank >= need, tf, 0.0)
                sc_ref[0, u] = jnp.where(drop > 0.0, -jnp.inf, x)
                return seen + jnp.sum(tf, axis=1, keepdims=True)

            lax.fori_loop(0, n_pages + 1, tie_blk, jnp.zeros((tq, 1), F32))


def _sample_index(page_table, iq2, w2, iknew_t, idx_pages_t, n_sel):
    b, n_pages = page_table.shape
    pps = min(IDX_PAGES_PER_STEP, n_pages)
    tq = iq2.shape[1] // N_HEADS_IDX
    nblk = -(-(n_pages + 1) // SUBLANES) * SUBLANES
    per_b = lambda r, w: pl.BlockSpec((1, r, w), lambda bi, s, pt: (bi, 0, 0))
    page = lambda j: pl.BlockSpec((None, IDX_DIM, PAGE_SIZE), lambda bi, s, pt, j=j: (pt[bi, s * pps + j], 0, 0))
    return pl.pallas_call(
        functools.partial(_sidx_kernel, pps=pps, n_pages=n_pages, n_sel=n_sel),
        grid_spec=pltpu.PrefetchScalarGridSpec(
            num_scalar_prefetch=1,
            grid=(b, n_pages // pps),
            in_specs=[per_b(N_HEADS_IDX * tq, IDX_DIM), per_b(N_HEADS_IDX * tq, LANES), per_b(IDX_DIM, PAGE_SIZE)]
            + [page(j) for j in range(pps)],
            out_specs=[pl.BlockSpec((1, nblk, tq, LANES), lambda bi, s, pt: (bi, 0, 0, 0)),
                       pl.BlockSpec((1, tq, LANES), lambda bi, s, pt: (bi, 0, 0))],
            scratch_shapes=[pltpu.VMEM((nblk, tq, LANES), I32)],
        ),
        out_shape=[jax.ShapeDtypeStruct((b, nblk, tq, LANES), F32), jax.ShapeDtypeStruct((b, tq, LANES), F32)],
        compiler_params=_params("parallel", "arbitrary"),
        name="sidx",
    )(page_table, iq2, w2, iknew_t, *([idx_pages_t] * pps))


def _sattn_kernel(pt_ref, qbd_ref, keys_ref, thr_ref, knew_ref, vnew_ref, *rest, ppb, n_pages):
    del pt_ref
    kpages, vpages = rest[:ppb], rest[ppb:2 * ppb]
    o_ref, m_scr, l_scr, acc_scr = rest[2 * ppb:]
    s = pl.program_id(1)
    nh = N_HEADS_A
    qbd = qbd_ref[0]
    tq = qbd.shape[0] // nh
    thr = jnp.concatenate([thr_ref[0, :, :1]] * nh, axis=0)

    @pl.when(s == 0)
    def _():
        m_scr[...] = jnp.full(m_scr.shape, NEG_BIG, F32)
        l_scr[...] = jnp.zeros(l_scr.shape, F32)
        acc_scr[...] = jnp.zeros(acc_scr.shape, F32)

    def pages_update(kts, vts, blks):
        kt = jnp.concatenate([a.astype(BF16) for a in kts], axis=1)
        vt = jnp.concatenate([a.astype(BF16) for a in vts], axis=1)
        kk = jnp.concatenate([keys_ref[0, blk] for blk in blks], axis=1)
        sc = jnp.dot(qbd, kt, preferred_element_type=F32)
        sc = jnp.where(jnp.concatenate([kk] * nh, axis=0) >= thr, sc, NEG_BIG)
        m_old = m_scr[:, :1]
        m_new = jnp.maximum(m_old, jnp.max(sc, axis=1, keepdims=True))
        alpha = jnp.exp2(m_old - m_new)
        p = jnp.exp2(sc - m_new)
        l_scr[...] = alpha * l_scr[...] + jnp.sum(p, axis=1, keepdims=True)
        pv = lax.dot_general(p.astype(BF16), vt, NT_DIMS, preferred_element_type=F32)
        acc_scr[...] = alpha * acc_scr[...] + pv
        m_scr[...] = jnp.broadcast_to(m_new, m_scr.shape)

    pages_update([r[...] for r in kpages], [r[...] for r in vpages], [s * ppb + j for j in range(ppb)])

    @pl.when(s == pl.num_programs(1) - 1)
    def _():
        pages_update([knew_ref[0]], [vnew_ref[0]], [n_pages])
        acc = acc_scr[...]
        o = jnp.concatenate([acc[h * tq:(h + 1) * tq, h * HEAD_DIM_A:(h + 1) * HEAD_DIM_A] for h in range(nh)],
                            axis=0)
        o_ref[0] = o / l_scr[:, :HEAD_DIM_A]


def _sample_attention(page_table, qbd, keys, thr, knew_t, vnew_t, k_pages_t, v_pages_t):
    b, n_pages = page_table.shape
    ppb = min(KV_PAGES_PER_STEP, n_pages)
    rows = qbd.shape[1]
    per_b3 = lambda r, w: pl.BlockSpec((1, r, w), lambda bi, s, pt: (bi, 0, 0))
    page = lambda j: pl.BlockSpec((None, WIDTH_A, PAGE_SIZE), lambda bi, s, pt, j=j: (pt[bi, s * ppb + j], 0, 0))
    return pl.pallas_call(
        functools.partial(_sattn_kernel, ppb=ppb, n_pages=n_pages),
        grid_spec=pltpu.PrefetchScalarGridSpec(
            num_scalar_prefetch=1,
            grid=(b, n_pages // ppb),
            in_specs=[per_b3(rows, WIDTH_A),
                      pl.BlockSpec((1,) + keys.shape[1:], lambda bi, s, pt: (bi, 0, 0, 0)),
                      per_b3(thr.shape[1], LANES), per_b3(WIDTH_A, PAGE_SIZE), per_b3(WIDTH_A, PAGE_SIZE)]
            + [page(j) for j in range(ppb)] * 2,
            out_specs=per_b3(rows, HEAD_DIM_A),
            scratch_shapes=[pltpu.VMEM((rows, LANES), F32), pltpu.VMEM((rows, LANES), F32),
                            pltpu.VMEM((rows, WIDTH_A), F32)],
        ),
        out_shape=jax.ShapeDtypeStruct((b, rows, HEAD_DIM_A), F32),
        compiler_params=_params("parallel", "arbitrary"),
        name="sattn",
    )(page_table, qbd, keys, thr, knew_t, vnew_t, *([k_pages_t] * ppb), *([v_pages_t] * ppb))


def _unmask_heads(xm, n_heads):
    x = xm.reshape(xm.shape[0], n_heads, LANES)
    return jnp.stack([x[:, h, (h % 2) * HEAD_DIM_A:(h % 2 + 1) * HEAD_DIM_A] for h in range(n_heads)], axis=1)


def _sample_layer(x_sample, cache_k, cache_v, cache_idx_k, state, page_table, w):
    b, t, d = x_sample.shape
    n = b * t
    n_pages = page_table.shape[1]
    past = n_pages * PAGE_SIZE
    pos = past + jnp.arange(t, dtype=I32)
    x1, p = _pre(x_sample.reshape(n, d), w, _rot_tables(jnp.tile(pos, b)), n, n, False)

    iq2 = _unmask_heads(p["iqm"], N_HEADS_IDX).reshape(b, t, N_HEADS_IDX, IDX_DIM)
    iq2 = iq2.transpose(0, 2, 1, 3).reshape(b, N_HEADS_IDX * t, IDX_DIM)
    w2 = p["ikw"][:, IDX_DIM:IDX_DIM + N_HEADS_IDX].reshape(b, t, N_HEADS_IDX).transpose(0, 2, 1)
    w2 = jnp.broadcast_to(w2.reshape(b, N_HEADS_IDX * t, 1), (b, N_HEADS_IDX * t, LANES))
    new_page = lambda a: jnp.pad(a.reshape(b, t, -1).transpose(0, 2, 1), ((0, 0), (0, 0), (0, PAGE_SIZE - t)))
    n_sel = min(TOPK_MAX, (past + t) // 4)
    keys, thr = _sample_index(page_table, iq2, w2, new_page(p["ik2"][:, :IDX_DIM]),
                              cache_idx_k[0].transpose(0, 2, 1), n_sel)

    q2 = _unmask_heads(p["qm"], N_HEADS_A).reshape(b, t, N_HEADS_A, HEAD_DIM_A).transpose(0, 2, 1, 3)
    eye = jnp.eye(N_HEADS_A, dtype=q2.dtype)
    qbd = (q2[:, :, :, None, :] * eye[None, :, None, :, None]).reshape(b, N_HEADS_A * t, WIDTH_A)
    pages_t = lambda c: c[0].transpose(0, 2, 3, 1).reshape(-1, WIDTH_A, PAGE_SIZE)
    o2 = _sample_attention(page_table, qbd, keys, thr, new_page(p["k"]), new_page(p["v"]),
                           pages_t(cache_k), pages_t(cache_v))
    attn = o2.reshape(b, N_HEADS_A, t, HEAD_DIM_A).transpose(0, 2, 1, 3).reshape(n, WIDTH_A)

    tp = -(-t // CHUNK_B) * CHUNK_B
    r3 = lambda a: jnp.pad(a.reshape(b, t, a.shape[-1]), ((0, 0), (0, tp - t), (0, 0)))
    o, s_new = _hgrn(r3(p["hq"]), r3(p["kf"]), r3(p["lf"]), r3(p["hi"]), state, tp)
    y = _post(x1, attn, o[:, :t].reshape(n, WIDTH_B), p, w, n, n)
    return (y.reshape(b, t, d),
            p["k"].reshape(b, t, N_HEADS_A, HEAD_DIM_A), p["v"].reshape(b, t, N_HEADS_A, HEAD_DIM_A),
            p["ikw"][:, :IDX_DIM].reshape(b, t, IDX_DIM), s_new)


def kernel(x_prompt, x_sample, cache_k, cache_v, cache_idx_k, state_hgrn, page_table,
           ffn1_norm, ffn1_w_gate, ffn1_w_up, ffn1_w_down, mix_norm, w_in, q_norm, k_norm,
           hgrn_lb, hgrn_o_norm, w_proj_attn, w_proj_hgrn, w_out,
           ffn2_norm, ffn2_w_gate, ffn2_w_up, ffn2_w_down):
    w = _prep_weights(ffn1_norm, ffn1_w_gate, ffn1_w_up, ffn1_w_down, mix_norm, w_in, q_norm, k_norm,
                      hgrn_lb, hgrn_o_norm, w_proj_attn, w_proj_hgrn, w_out,
                      ffn2_norm, ffn2_w_gate, ffn2_w_up, ffn2_w_down)
    yp, kp, vp, ikp, sp = _prompt_layer(x_prompt, w)
    ys, ks, vs, iks, ss = _sample_layer(x_sample, cache_k, cache_v, cache_idx_k, state_hgrn[0], page_table, w)
    return (yp, ys, kp[None], vp[None], ikp[None], sp[None], ks[None], vs[None], iks[None], ss[None])
```

```python
import functools

import numpy as np
import jax
import jax.numpy as jnp
from jax import lax
from jax.experimental import pallas as pl
from jax.experimental.pallas import tpu as pltpu

F32 = jnp.float32
BF16 = jnp.bfloat16
I32 = jnp.int32

D_MODEL = 1024
PAST_LEN = 16384
PAGE_SIZE = 128
N_HEADS_A = 8
HEAD_DIM_A = 64
ROT_DIM = HEAD_DIM_A // 4
ROPE_THETA = 500000.0
N_HEADS_IDX = 4
IDX_DIM = 64
IDX_W_SCALE = (N_HEADS_IDX * IDX_DIM) ** -0.5
TOPK_MAX = 256
N_HEADS_B = 4
KEY_DIM_B = 128
VAL_DIM_B = 128
CHUNK_B = 16
WIDTH_A = N_HEADS_A * HEAD_DIM_A
WIDTH_B = N_HEADS_B * VAL_DIM_B
KEYW_B = N_HEADS_B * KEY_DIM_B
EPS = 1e-6
IN_SIZES = (WIDTH_A, WIDTH_A, WIDTH_A, N_HEADS_IDX * IDX_DIM, IDX_DIM, N_HEADS_IDX,
            KEYW_B, KEYW_B, WIDTH_B, WIDTH_B, D_MODEL, D_MODEL)
IN_OFFSETS = tuple(int(o) for o in np.cumsum(IN_SIZES)[:-1])

LANES = 128
SUBLANES = 8
VMEM_LIMIT_BYTES = 56 * 1024 * 1024
INT_MIN = np.int32(-2 ** 31)
NEG_BIG = -1e30
FF_CHUNK = 256
IN_CHUNK = 512
NT_DIMS = (((1,), (1,)), ((), ()))
LOG2E = 1.4426950408889634
ATTN_TILE = 256


def _params(*sem):
    return pltpu.CompilerParams(dimension_semantics=sem, vmem_limit_bytes=VMEM_LIMIT_BYTES)


def _float_key(x):
    u = pltpu.bitcast(x + 0.0, I32)
    return u ^ ((u >> 31) & np.int32(0x7FFFFFFF))


def _ffn_kernel(x_ref, g_ref, wg_ref, wu_ref, wd_ref, o_ref, h_scr, acc_scr):
    x = x_ref[...]
    ms = jnp.mean(x * x, axis=-1, keepdims=True)
    h_scr[...] = (x * lax.rsqrt(ms + EPS) * g_ref[...]).astype(BF16)
    acc_scr[...] = jnp.zeros_like(acc_scr)

    def chunk(c, carry):
        h = h_scr[...]
        cs = pl.ds(pl.multiple_of(c * FF_CHUNK, FF_CHUNK), FF_CHUNK)
        g = jnp.dot(h, wg_ref[:, cs], preferred_element_type=F32)
        u = jnp.dot(h, wu_ref[:, cs], preferred_element_type=F32)
        a = (g * jax.nn.sigmoid(g) * u).astype(BF16)
        acc_scr[...] += jnp.dot(a, wd_ref[cs, :], preferred_element_type=F32)
        return carry

    lax.fori_loop(0, wg_ref.shape[1] // FF_CHUNK, chunk, 0)
    o_ref[...] = x_ref[...] + 0.5 * acc_scr[...]


def _resident(shape):
    return pl.BlockSpec(shape, lambda i: (0,) * len(shape), pipeline_mode=pl.Buffered(1))


def _ffn(x, gain, wg, wu, wd, tm):
    n, d = x.shape
    return pl.pallas_call(
        _ffn_kernel,
        grid=(n // tm,),
        in_specs=[pl.BlockSpec((tm, d), lambda i: (i, 0)), _resident((1, d)),
                  _resident(wg.shape), _resident(wu.shape), _resident(wd.shape)],
        out_specs=pl.BlockSpec((tm, d), lambda i: (i, 0)),
        out_shape=jax.ShapeDtypeStruct((n, d), F32),
        scratch_shapes=[pltpu.VMEM((tm, d), BF16), pltpu.VMEM((tm, d), F32)],
        compiler_params=_params("parallel"),
        name="ffn",
    )(x, gain.reshape(1, d), wg, wu, wd)


_J_Q, _J_K, _J_V, _J_IDX, _J_HQ, _J_HF, _J_HI, _J_HG, _J_GA0, _J_GA1, _J_GB0, _J_GB1 = range(12)


def _rot(xc, c, sa, sb):
    return (xc * c + pltpu.roll(xc, LANES - ROT_DIM // 2, 1) * sa
            + pltpu.roll(xc, ROT_DIM // 2, 1) * sb)


def _inproj_kernel(x_ref, g_ref, w_ref, bd_ref, qg_ref, kg_ref, lb_ref, cos_ref, sa_ref, sb_ref,
                   qm_ref, k_ref, kb_ref, v_ref, vt_ref, iqm_ref, ikw_ref, ik2_ref,
                   hq_ref, kf_ref, lf_ref, hi_ref, hg_ref, ga_ref, gb_ref, h_scr, *, kv_cols):
    tm = x_ref.shape[0]
    x = x_ref[...]
    ms = jnp.mean(x * x, axis=-1, keepdims=True)
    h_scr[...] = (x * lax.rsqrt(ms + EPS) * g_ref[...]).astype(BF16)

    def chunk(jj):
        def run(fn):
            fn(jnp.dot(h_scr[...], w_ref[jj], preferred_element_type=F32))
        return run

    lane = lax.broadcasted_iota(I32, (tm, LANES), 1)
    lo_half = lane < HEAD_DIM_A

    def head_norm(t, gain):
        ms = jnp.dot((t * t).astype(BF16), bd_ref[...], preferred_element_type=F32)
        return t * lax.rsqrt(ms + EPS) * gain

    @chunk(_J_Q)
    def _(y):
        yn = head_norm(y, qg_ref[...])
        c, sa, sb = cos_ref[...], sa_ref[...], sb_ref[...]
        for p in range(WIDTH_A // LANES):
            r = _rot(yn[:, p * LANES:(p + 1) * LANES], c, sa, sb) * (HEAD_DIM_A ** -0.5 * LOG2E)
            qm_ref[:, (2 * p) * LANES:(2 * p + 1) * LANES] = jnp.where(lo_half, r, 0.0).astype(BF16)
            qm_ref[:, (2 * p + 1) * LANES:(2 * p + 2) * LANES] = jnp.where(lo_half, 0.0, r).astype(BF16)

    @chunk(_J_K)
    def _(y):
        yn = head_norm(y, kg_ref[...])
        c, sa, sb = cos_ref[...], sa_ref[...], sb_ref[...]
        for p in range(WIDTH_A // LANES):
            sl = slice(p * LANES, (p + 1) * LANES)
            r = _rot(yn[:, sl], c, sa, sb)
            if kv_cols:
                k_ref[0, sl, :] = r.T
            else:
                k_ref[:, sl] = r
            kb_ref[:, sl] = r.astype(BF16)

    @chunk(_J_V)
    def _(y):
        yt = y.T
        if kv_cols:
            v_ref[0] = yt
        else:
            v_ref[...] = y
        for u in range(vt_ref.shape[0]):
            vt_ref[u] = yt[:, u * vt_ref.shape[2]:(u + 1) * vt_ref.shape[2]].astype(BF16)

    @chunk(_J_IDX)
    def _(y):
        c, sa, sb = cos_ref[...], sa_ref[...], sb_ref[...]
        for p in range(2):
            r = _rot(y[:, p * LANES:(p + 1) * LANES], c, sa, sb)
            iqm_ref[:, (2 * p) * LANES:(2 * p + 1) * LANES] = jnp.where(lo_half, r, 0.0).astype(BF16)
            iqm_ref[:, (2 * p + 1) * LANES:(2 * p + 2) * LANES] = jnp.where(lo_half, 0.0, r).astype(BF16)
        r = _rot(y[:, 2 * LANES:3 * LANES], jnp.where(lo_half, c, 1.0),
                 jnp.where(lo_half, sa, 0.0), jnp.where(lo_half, sb, 0.0))
        if kv_cols:
            ikw_ref[0] = r.T
        else:
            ikw_ref[...] = r
        ik2_ref[...] = jnp.where(lo_half, r, pltpu.roll(r, HEAD_DIM_A, 1)).astype(BF16)

    @chunk(_J_HQ)
    def _(y):
        hq_ref[...] = (y * jax.nn.sigmoid(y)).astype(BF16)

    @chunk(_J_HF)
    def _(y):
        lb = lb_ref[...]
        lf_ref[...] = jnp.log(lb + (1.0 - lb) * jax.nn.sigmoid(y))
        kf_ref[...] = ((1.0 - lb) * jax.nn.sigmoid(-y)).astype(BF16)

    @chunk(_J_HI)
    def _(y):
        hi_ref[...] = y.astype(BF16)

    @chunk(_J_HG)
    def _(y):
        hg_ref[...] = (y * jax.nn.sigmoid(y)).astype(BF16)

    for jj, ref, half in ((_J_GA0, ga_ref, 0), (_J_GA1, ga_ref, 1), (_J_GB0, gb_ref, 0), (_J_GB1, gb_ref, 1)):
        @chunk(jj)
        def _(y, ref=ref, half=half):
            ref[:, half * IN_CHUNK:(half + 1) * IN_CHUNK] = jax.nn.sigmoid(y).astype(BF16)


def _inproj(x, gain, w12, bd, qg, kg, lb, cos_t, sa_t, sb_t, tm, kv_cols):
    n, d = x.shape
    nt = cos_t.shape[0] // tm
    row = lambda w: pl.BlockSpec((tm, w), lambda i: (i, 0))
    tab = pl.BlockSpec((tm, LANES), lambda i: (i % nt, 0))
    vt_tile = min(ATTN_TILE, tm)

    def rows(w, dt):
        return row(w), jax.ShapeDtypeStruct((n, w), dt)

    def kv(w):
        if kv_cols:
            return (pl.BlockSpec((1, w, tm), lambda i: (i // nt, 0, i % nt)),
                    jax.ShapeDtypeStruct((n // (nt * tm), w, nt * tm), F32))
        return rows(w, F32)

    outs = [rows(2 * WIDTH_A, BF16), kv(WIDTH_A), rows(WIDTH_A, BF16), kv(WIDTH_A),
            (pl.BlockSpec((tm // vt_tile, WIDTH_A, vt_tile), lambda i: (i, 0, 0)),
             jax.ShapeDtypeStruct((n // vt_tile, WIDTH_A, vt_tile), BF16)),
            rows(2 * N_HEADS_IDX * IDX_DIM, BF16), kv(LANES), rows(LANES, BF16),
            rows(KEYW_B, BF16), rows(KEYW_B, BF16), rows(KEYW_B, F32), rows(WIDTH_B, BF16), rows(WIDTH_B, BF16),
            rows(D_MODEL, BF16), rows(D_MODEL, BF16)]
    return pl.pallas_call(
        functools.partial(_inproj_kernel, kv_cols=kv_cols),
        grid=(n // tm,),
        in_specs=[row(d), _resident((1, d)), _resident(w12.shape), _resident(bd.shape),
                  _resident(qg.shape), _resident(kg.shape), _resident(lb.shape), tab, tab, tab],
        out_specs=[o[0] for o in outs],
        out_shape=[o[1] for o in outs],
        scratch_shapes=[pltpu.VMEM((tm, d), BF16)],
        compiler_params=_params("parallel"),
        name="inproj",
    )(x, gain.reshape(1, d), w12, bd, qg, kg, lb, cos_t, sa_t, sb_t)


def _tree(parts, op):
    while len(parts) > 1:
        parts = [op(a, b) for a, b in zip(parts[0::2], parts[1::2])] + ([parts[-1]] if len(parts) % 2 else [])
    return parts[0]


def _tree_sum(parts):
    return _tree(parts, jnp.add)


def _refine_threshold(cand, n_sel, count_fn, next_above, next_below):
    neg_flt_max_key = np.int32(-2139095040)
    key = jnp.maximum(cand, neg_flt_max_key)
    t = pltpu.bitcast(key ^ ((key >> 31) & np.int32(0x7FFFFFFF)), F32)

    def state(t):
        cge, cgt = count_fn(t, False), count_fn(t, True)
        lower = jnp.where(cand == INT_MIN, 0.0, -1.0)
        return cge, cgt, jnp.where(cgt >= n_sel, 1.0, jnp.where(cge < n_sel, lower, 0.0))

    def body(st):
        t, _, _, move = st
        t = jnp.where(move > 0.0, next_above(t), jnp.where(move < 0.0, next_below(t), t))
        return (t,) + state(t)

    t, cge, cgt, _ = lax.while_loop(lambda st: jnp.max(jnp.abs(st[3])) > 0.0, body, (t,) + state(t))
    return t, cge, cgt


def _search_threshold(count_ge, shape, n_sel):
    c0 = count_ge(jnp.zeros(shape, I32))
    ok = c0 >= n_sel
    cand = jnp.where(ok, np.int32(0), INT_MIN)
    ccnt = jnp.where(ok, c0, 0.0)

    def bit_body(i, carry):
        cand, ccnt = carry
        t = cand + (jnp.int32(1) << (30 - i))
        c = count_ge(t)
        ok = c >= n_sel
        return jnp.where(ok, t, cand), jnp.where(ok, c, ccnt)

    return lax.fori_loop(0, 31, bit_body, (cand, ccnt))


def _pattn_kernel(iqm_ref, ikw_ref, ik2_ref, qm_ref, k_ref, vt_ref, o_ref,
                  keys_scr, sc_scr, m_scr, a_scr, acc_scr, s_scr, p_scr, *, n_sel, tq):
    qi = pl.program_id(1)
    nkb = qi + 1
    krow = lax.broadcasted_iota(I32, (tq, tq), 0)
    qcol = lax.broadcasted_iota(I32, (tq, tq), 1)
    half = HEAD_DIM_A

    ikw_t = ikw_ref[0]
    wrow = [ikw_t[IDX_DIM + h:IDX_DIM + h + 1, :] * IDX_W_SCALE for h in range(N_HEADS_IDX)]

    def score_blk(kb, c):
        ik2 = ik2_ref[0, pl.ds(pl.multiple_of(kb * tq, tq), tq), :]
        sc = jnp.zeros((tq, tq), F32)
        for h in range(N_HEADS_IDX):
            d = lax.dot_general(ik2, iqm_ref[0, :, h * LANES:(h + 1) * LANES], NT_DIMS,
                                preferred_element_type=F32)
            sc = sc + jnp.maximum(d, 0.0) * wrow[h]
        off = jnp.where(kb < qi, jnp.int32(tq), jnp.int32(0))
        adm = krow <= qcol + off
        keys_scr[kb] = jnp.where(adm, _float_key(sc), INT_MIN)
        sc_scr[kb] = jnp.where(adm, sc, -jnp.inf)
        return c

    lax.fori_loop(0, nkb, score_blk, 0)

    def over_rows(tile_fn, op, init):
        def body(kb, acc):
            parts = [tile_fn(kb, slice(r * SUBLANES, (r + 1) * SUBLANES)) for r in range(tq // SUBLANES)]
            return op(acc, _tree(parts, op))
        return lax.fori_loop(0, nkb, body, jnp.full((SUBLANES, tq), init, F32))

    def count_ge(t):
        acc = over_rows(lambda kb, rs: jnp.where(keys_scr[kb, rs, :] >= t, 1.0, 0.0), jnp.add, 0.0)
        return jnp.sum(acc, axis=0, keepdims=True)

    cand, _ = _search_threshold(count_ge, (1, tq), n_sel)

    def fcount(t, strict):
        hit = (lambda s: s > t) if strict else (lambda s: s >= t)
        acc = over_rows(lambda kb, rs: jnp.where(hit(sc_scr[kb, rs, :]), 1.0, 0.0), jnp.add, 0.0)
        return jnp.sum(acc, axis=0, keepdims=True)

    def next_above(t):
        def tile(kb, rs):
            s = sc_scr[kb, rs, :]
            return jnp.where(s > t, s, jnp.inf)
        return jnp.min(over_rows(tile, jnp.minimum, jnp.inf), axis=0, keepdims=True)

    def next_below(t):
        def tile(kb, rs):
            s = sc_scr[kb, rs, :]
            return jnp.where(s < t, s, -jnp.inf)
        return jnp.max(over_rows(tile, jnp.maximum, -jnp.inf), axis=0, keepdims=True)

    thr, cge, cgt = _refine_threshold(cand, n_sel, fcount, next_above, next_below)

    @pl.when(jnp.max(cge) > n_sel)
    def _():
        need = n_sel - cgt
        lower = jnp.where(qcol < krow, 1.0, 0.0).astype(BF16)

        def tie_blk(kb, seen):
            s = sc_scr[kb]
            tf = jnp.where(s == thr, 1.0, 0.0)
            rank = jnp.dot(lower, tf.astype(BF16), preferred_element_type=F32) + seen
            drop = jnp.where(rank >= need, tf, 0.0)
            sc_scr[kb] = jnp.where(drop > 0.0, -jnp.inf, s)
            return seen + jnp.sum(tf, axis=0, keepdims=True)

        lax.fori_loop(0, nkb, tie_blk, jnp.zeros((1, tq), F32))

    m_scr[...] = jnp.full(m_scr.shape, NEG_BIG, F32)
    acc_scr[...] = jnp.zeros(acc_scr.shape, F32)
    ones_half = jnp.ones((half, tq), BF16)

    def attn_blk(kb, c):
        bias = jnp.where(sc_scr[kb] >= thr, 0.0, NEG_BIG)
        ks = pl.ds(pl.multiple_of(kb * tq, tq), tq)
        for h in range(N_HEADS_A):
            pr = h // 2
            s_scr[h] = lax.dot_general(k_ref[0, ks, pr * LANES:(pr + 1) * LANES],
                                       qm_ref[0, :, h * LANES:(h + 1) * LANES], NT_DIMS,
                                       preferred_element_type=F32) + bias
        for h in range(N_HEADS_A):
            s = s_scr[h]
            m_old = m_scr[h]
            m_new = jnp.maximum(m_old, jnp.max(s, axis=0, keepdims=True))
            a_scr[h] = jnp.exp2(m_old - m_new)
            p_scr[h] = jnp.exp2(s - m_new).astype(BF16)
            m_scr[h] = m_new
        for h in range(N_HEADS_A):
            pr = h // 2
            vt = vt_ref[0, kb, pr * LANES:(pr + 1) * LANES, :]
            lhs = (jnp.concatenate([vt[:half], ones_half], axis=0) if h % 2 == 0
                   else jnp.concatenate([ones_half, vt[half:]], axis=0))
            acc_scr[h] = a_scr[h] * acc_scr[h] + jnp.dot(lhs, p_scr[h], preferred_element_type=F32)
        return c

    lax.fori_loop(0, nkb, attn_blk, 0)
    for pr in range(N_HEADS_A // 2):
        ae, ao = acc_scr[2 * pr], acc_scr[2 * pr + 1]
        ot = jnp.concatenate([ae[:half] / ae[half:], ao[half:] / ao[:half]], axis=0)
        o_ref[0, :, pr * LANES:(pr + 1) * LANES] = ot.T


def _prompt_attention(iqm, ikw, ik2, qm, kb, vt, n_sel, tq):
    b, t, _ = qm.shape
    nq = t // tq
    qblk = lambda w: pl.BlockSpec((1, tq, w), lambda bi, qi: (bi, qi, 0))
    full = lambda w: pl.BlockSpec((1, t, w), lambda bi, qi: (bi, 0, 0))
    return pl.pallas_call(
        functools.partial(_pattn_kernel, n_sel=n_sel, tq=tq),
        grid=(b, nq),
        in_specs=[qblk(iqm.shape[2]), pl.BlockSpec((1, LANES, tq), lambda bi, qi: (bi, 0, qi)), full(LANES),
                  qblk(qm.shape[2]), full(WIDTH_A),
                  pl.BlockSpec((1, nq, WIDTH_A, tq), lambda bi, qi: (bi, 0, 0, 0))],
        out_specs=qblk(WIDTH_A),
        out_shape=jax.ShapeDtypeStruct((b, t, WIDTH_A), F32),
        scratch_shapes=[
            pltpu.VMEM((nq, tq, tq), I32),
            pltpu.VMEM((nq, tq, tq), F32),
            pltpu.VMEM((N_HEADS_A, 1, tq), F32),
            pltpu.VMEM((N_HEADS_A, 1, tq), F32),
            pltpu.VMEM((N_HEADS_A, LANES, tq), F32),
            pltpu.VMEM((N_HEADS_A, tq, tq), F32),
            pltpu.VMEM((N_HEADS_A, tq, tq), BF16),
        ],
        compiler_params=_params("parallel", "arbitrary"),
        name="pattn",
    )(iqm, ikw, ik2, qm, kb, vt)


def _hgrn_kernel(q_ref, k_ref, g_ref, v_ref, s0_ref, o_ref, sn_ref, s_scr, *, tc):
    ti = pl.program_id(1)
    c_ = CHUNK_B

    streams = [(bi, h) for bi in range(q_ref.shape[0]) for h in range(N_HEADS_B)]

    @pl.when(ti == 0)
    def _():
        for si, (bi, h) in enumerate(streams):
            s_scr[si] = s0_ref[bi, h].T

    rowi = lax.broadcasted_iota(I32, (c_, LANES), 0)
    lanei = lax.broadcasted_iota(I32, (c_, LANES), 1)

    def chunk(ci, carry):
        rs = pl.ds(pl.multiple_of(ci * c_, c_), c_)
        for si, (bi, h) in enumerate(streams):
            hs = slice(h * LANES, (h + 1) * LANES)
            q = q_ref[bi, rs, hs].astype(F32)
            k = k_ref[bi, rs, hs].astype(F32)
            v = v_ref[bi, rs, hs]
            b = g_ref[bi, rs, hs] * LOG2E
            for sh in (1, 2, 4, 8):
                b = b + jnp.where(rowi >= sh, pltpu.roll(b, sh, 0), 0.0)
            b_last = b[c_ - 1:c_, :]
            st_old = s_scr[si]
            o = lax.dot_general((q * jnp.exp2(b)).astype(BF16), st_old.astype(BF16), NT_DIMS,
                                preferred_element_type=F32)
            cols = []
            for t in range(c_):
                e = jnp.exp2(jnp.minimum(b[t:t + 1, :] - b, 0.0))
                col = jnp.sum(e * k * q[t:t + 1, :], axis=1, keepdims=True)
                cols.append(jnp.where(lanei == t, col, 0.0))
            att = jnp.where(rowi <= lanei, _tree_sum(cols), 0.0).T[:c_]
            o_ref[bi, rs, hs] = o + jnp.dot(att.astype(BF16), v, preferred_element_type=F32)
            kd = k * jnp.exp2(b_last - b)
            upd = jnp.dot(v.astype(F32).T.astype(BF16), kd.astype(BF16), preferred_element_type=F32)
            s_scr[si] = st_old * jnp.exp2(b_last) + upd
        return carry

    lax.fori_loop(0, tc // c_, chunk, 0)

    @pl.when(ti == pl.num_programs(1) - 1)
    def _():
        for si, (bi, h) in enumerate(streams):
            sn_ref[bi, h] = s_scr[si].T


HGRN_BATCH_GROUP = 4


def _hgrn(q, k, g, v, s0, tc):
    b, t, w = q.shape
    bg = min(HGRN_BATCH_GROUP, b)
    blk = pl.BlockSpec((bg, tc, w), lambda bi, ti: (bi, ti, 0))
    st = pl.BlockSpec((bg, N_HEADS_B, KEY_DIM_B, VAL_DIM_B), lambda bi, ti: (bi, 0, 0, 0))
    return pl.pallas_call(
        functools.partial(_hgrn_kernel, tc=tc),
        grid=(b // bg, t // tc),
        in_specs=[blk, blk, blk, blk, st],
        out_specs=[blk, st],
        out_shape=[jax.ShapeDtypeStruct((b, t, w), F32), jax.ShapeDtypeStruct(s0.shape, F32)],
        scratch_shapes=[pltpu.VMEM((bg * N_HEADS_B, KEY_DIM_B, VAL_DIM_B), F32)],
        compiler_params=_params("parallel", "arbitrary"),
        name="hgrn",
    )(q, k, g, v, s0)


def _merge_kernel(attn_ref, o_ref, hg_ref, ga_ref, gb_ref, x_ref, on_ref, wpa_ref, wph_ref, wo_ref, y_ref):
    o = o_ref[...]
    parts = []
    for h in range(N_HEADS_B):
        oh = o[:, h * LANES:(h + 1) * LANES]
        ms = jnp.mean(oh * oh, axis=-1, keepdims=True)
        parts.append(oh * lax.rsqrt(ms + EPS))
    on = jnp.concatenate(parts, axis=1) * on_ref[...] * hg_ref[...]
    pa = jnp.dot(attn_ref[...].astype(BF16), wpa_ref[...], preferred_element_type=F32)
    ph = jnp.dot(on.astype(BF16), wph_ref[...], preferred_element_type=F32)
    merged = ga_ref[...] * pa + gb_ref[...] * ph
    y_ref[...] = x_ref[...] + jnp.dot(merged.astype(BF16), wo_ref[...], preferred_element_type=F32)


def _merge(attn, o, hg, ga, gb, x, onorm, wpa, wph, wo, tm):
    n, d = x.shape
    row = lambda w: pl.BlockSpec((tm, w), lambda i: (i, 0))
    const = lambda r, w: pl.BlockSpec((r, w), lambda i: (0, 0))
    return pl.pallas_call(
        _merge_kernel,
        grid=(n // tm,),
        in_specs=[row(WIDTH_A), row(WIDTH_B), row(WIDTH_B), row(d), row(d), row(d),
                  const(1, WIDTH_B), const(WIDTH_A, d), const(WIDTH_B, d), const(d, d)],
        out_specs=row(d),
        out_shape=jax.ShapeDtypeStruct((n, d), F32),
        compiler_params=_params("parallel"),
        name="merge",
    )(attn, o, hg, ga, gb, x, onorm, wpa, wph, wo)


def _rot_tables(pos):
    r = pos.shape[0]
    inv = ROPE_THETA ** (-jnp.arange(0, ROT_DIM, 2, dtype=F32) / ROT_DIM)
    ang = pos.astype(F32)[:, None] * inv[None, :]
    cos, sin = jnp.cos(ang), jnp.sin(ang)
    half = ROT_DIM // 2
    rest = HEAD_DIM_A - ROT_DIM
    one, z8, zr = jnp.ones((r, rest), F32), jnp.zeros((r, half), F32), jnp.zeros((r, rest), F32)
    c64 = jnp.concatenate([cos, cos, one], axis=1)
    a64 = jnp.concatenate([-sin, z8, zr], axis=1)
    b64 = jnp.concatenate([z8, sin, zr], axis=1)
    return tuple(jnp.concatenate([t, t], axis=1) for t in (c64, a64, b64))


def _prep_weights(ffn1_norm, ffn1_w_gate, ffn1_w_up, ffn1_w_down, mix_norm, w_in, q_norm, k_norm,
                  hgrn_lb, hgrn_o_norm, w_proj_attn, w_proj_hgrn, w_out,
                  ffn2_norm, ffn2_w_gate, ffn2_w_up, ffn2_w_down):
    l = 0
    q, k, v, iq, ik, iw, hq, hf, hi, hg, ga, gb = jnp.split(w_in[l], IN_OFFSETS, axis=1)
    pad = jnp.zeros((D_MODEL, IN_CHUNK - iq.shape[1] - ik.shape[1] - iw.shape[1]), F32)
    idx = jnp.concatenate([iq, ik, iw, pad], axis=1)
    w12 = jnp.stack([q, k, v, idx, hq, hf, hi, hg,
                     ga[:, :IN_CHUNK], ga[:, IN_CHUNK:], gb[:, :IN_CHUNK], gb[:, IN_CHUNK:]]).astype(BF16)
    head = np.arange(WIDTH_A) // HEAD_DIM_A
    bd = jnp.asarray((head[:, None] == head[None, :]).astype(np.float32) / HEAD_DIM_A, BF16)
    lb = jnp.cumsum(jax.nn.softmax(hgrn_lb.astype(F32), axis=0), axis=0)[l].reshape(1, KEYW_B)
    cols = rows = lambda a: a.astype(BF16)
    return dict(
        ffn1=(ffn1_norm[l], cols(ffn1_w_gate[l]), cols(ffn1_w_up[l]), rows(ffn1_w_down[l])),
        ffn2=(ffn2_norm[l], cols(ffn2_w_gate[l]), cols(ffn2_w_up[l]), rows(ffn2_w_down[l])),
        mix_norm=mix_norm[l], w12=w12, bd=bd,
        qg=jnp.tile(q_norm[l], N_HEADS_A).reshape(1, WIDTH_A),
        kg=jnp.tile(k_norm[l], N_HEADS_A).reshape(1, WIDTH_A),
        lb=lb,
        onorm=jnp.tile(hgrn_o_norm[l], N_HEADS_B).reshape(1, WIDTH_B),
        wpa=w_proj_attn[l].astype(BF16), wph=w_proj_hgrn[l].astype(BF16), wo=w_out[l].astype(BF16),
    )


def _pre(x, w, tabs, tm_ffn, tm_in, kv_cols):
    x1 = _ffn(x, *w["ffn1"], tm_ffn)
    names = ("qm", "k", "kb", "v", "vt", "iqm", "ikw", "ik2", "hq", "kf", "lf", "hi", "hg", "ga", "gb")
    outs = _inproj(x1, w["mix_norm"], w["w12"], w["bd"], w["qg"], w["kg"], w["lb"], *tabs, tm_in, kv_cols)
    return x1, dict(zip(names, outs))


def _post(x1, attn, o, p, w, tm_merge, tm_ffn):
    x2 = _merge(attn, o, p["hg"], p["ga"], p["gb"], x1, w["onorm"], w["wpa"], w["wph"], w["wo"], tm_merge)
    return _ffn(x2, *w["ffn2"], tm_ffn)


def _prompt_layer(x_prompt, w):
    b, t, d = x_prompt.shape
    n = b * t
    tabs = _rot_tables(jnp.arange(t, dtype=I32))
    x1, p = _pre(x_prompt.reshape(n, d), w, tabs, min(1024, n), min(512, t), True)
    r3 = lambda a: a.reshape(b, t, a.shape[-1])
    n_sel = min(TOPK_MAX, t // 4)
    tq = min(ATTN_TILE, t)
    vt = p["vt"].reshape(b, t // tq, WIDTH_A, tq)
    attn = _prompt_attention(r3(p["iqm"]), p["ikw"], r3(p["ik2"]), r3(p["qm"]), r3(p["kb"]), vt, n_sel, tq)
    s0 = jnp.zeros((b, N_HEADS_B, KEY_DIM_B, VAL_DIM_B), F32)
    o, s_new = _hgrn(r3(p["hq"]), r3(p["kf"]), r3(p["lf"]), r3(p["hi"]), s0, min(256, t))
    y = _post(x1, attn.reshape(n, WIDTH_A), o.reshape(n, WIDTH_B), p, w, min(512, n), min(1024, n))
    heads = lambda a: a.reshape(b, N_HEADS_A, HEAD_DIM_A, t).transpose(0, 3, 1, 2)
    return (y.reshape(b, t, d), heads(p["k"]), heads(p["v"]),
            p["ikw"][:, :IDX_DIM, :].transpose(0, 2, 1), s_new)


IDX_PAGES_PER_STEP = 128
KV_PAGES_PER_STEP = 16


def _sidx_kernel(pt_ref, iq2_ref, w2_ref, iknew_ref, *rest, pps, n_pages, n_sel):
    del pt_ref
    pages = rest[:pps]
    sc_ref, thr_ref, keys_scr = rest[pps:]
    s = pl.program_id(1)
    nblk = sc_ref.shape[1]
    iq2 = iq2_ref[0]
    w2 = w2_ref[0] * IDX_W_SCALE
    tq = iq2.shape[0] // N_HEADS_IDX

    def block_scores(ik_t):
        d = jnp.dot(iq2, ik_t, preferred_element_type=F32)
        r = jnp.maximum(d, 0.0) * w2
        sc = r[0:tq]
        for h in range(1, N_HEADS_IDX):
            sc = sc + r[h * tq:(h + 1) * tq]
        return sc

    for j in range(pps):
        sc = block_scores(pages[j][...].astype(BF16))
        sc_ref[0, s * pps + j] = sc
        keys_scr[s * pps + j] = _float_key(sc)

    @pl.when(s == pl.num_programs(1) - 1)
    def _():
        lane = lax.broadcasted_iota(I32, (tq, LANES), 1)
        row = lax.broadcasted_iota(I32, (tq, LANES), 0)
        sc = block_scores(iknew_ref[0])
        sc_ref[0, n_pages] = jnp.where(lane <= row, sc, -jnp.inf)
        keys_scr[n_pages] = jnp.where(lane <= row, _float_key(sc), INT_MIN)
        for u in range(n_pages + 1, nblk):
            sc_ref[0, u] = jnp.full((tq, LANES), -jnp.inf, F32)
            keys_scr[u] = jnp.full((tq, LANES), INT_MIN, I32)

        def over_blocks(tile_fn, op):
            return _tree([tile_fn(u) for u in range(nblk)], op)

        def count_ge(t):
            return jnp.sum(over_blocks(lambda u: jnp.where(keys_scr[u] >= t, 1.0, 0.0), jnp.add),
                           axis=1, keepdims=True)

        def fcount(t, strict):
            hit = (lambda x: x > t) if strict else (lambda x: x >= t)
            return jnp.sum(over_blocks(lambda u: jnp.where(hit(sc_ref[0, u]), 1.0, 0.0), jnp.add),
                           axis=1, keepdims=True)

        def next_above(t):
            def tile(u):
                x = sc_ref[0, u]
                return jnp.where(x > t, x, jnp.inf)
            return jnp.min(over_blocks(tile, jnp.minimum), axis=1, keepdims=True)

        def next_below(t):
            def tile(u):
                x = sc_ref[0, u]
                return jnp.where(x < t, x, -jnp.inf)
            return jnp.max(over_blocks(tile, jnp.maximum), axis=1, keepdims=True)

        cand, _ = _search_threshold(count_ge, (tq, 1), n_sel)
        thr, cge, cgt = _refine_threshold(cand, n_sel, fcount, next_above, next_below)
        thr_ref[0] = jnp.broadcast_to(thr, (tq, LANES))

        @pl.when(jnp.max(cge) > n_sel)
        def _():
            need = n_sel - cgt
            r2 = lax.broadcasted_iota(I32, (LANES, LANES), 0)
            c2 = lax.broadcasted_iota(I32, (LANES, LANES), 1)
            upper = jnp.where(r2 < c2, 1.0, 0.0)

            def tie_blk(u, seen):
                x = sc_ref[0, u]
                tf = jnp.where(x == thr, 1.0, 0.0)
                rank = jnp.dot(tf, upper, preferred_element_type=F32) + seen
                drop = jnp.where(rank >= need, tf, 0.0)
                sc_ref[0, u] = jnp.where(drop > 0.0, -jnp.inf, x)
                return seen + jnp.sum(tf, axis=1, keepdims=True)

            lax.fori_loop(0, n_pages + 1, tie_blk, jnp.zeros((tq, 1), F32))


def _sample_index(page_table, iq2, w2, iknew_t, idx_pages_t, n_sel):
    b, n_pages = page_table.shape
    pps = min(IDX_PAGES_PER_STEP, n_pages)
    tq = iq2.shape[1] // N_HEADS_IDX
    nblk = -(-(n_pages + 1) // SUBLANES) * SUBLANES
    per_b = lambda r, w: pl.BlockSpec((1, r, w), lambda bi, s, pt: (bi, 0, 0))
    page = lambda j: pl.BlockSpec((None, IDX_DIM, PAGE_SIZE), lambda bi, s, pt, j=j: (pt[bi, s * pps + j], 0, 0))
    return pl.pallas_call(
        functools.partial(_sidx_kernel, pps=pps, n_pages=n_pages, n_sel=n_sel),
        grid_spec=pltpu.PrefetchScalarGridSpec(
            num_scalar_prefetch=1,
            grid=(b, n_pages // pps),
            in_specs=[per_b(N_HEADS_IDX * tq, IDX_DIM), per_b(N_HEADS_IDX * tq, LANES), per_b(IDX_DIM, PAGE_SIZE)]
            + [page(j) for j in range(pps)],
            out_specs=[pl.BlockSpec((1, nblk, tq, LANES), lambda bi, s, pt: (bi, 0, 0, 0)),
                       pl.BlockSpec((1, tq, LANES), lambda bi, s, pt: (bi, 0, 0))],
            scratch_shapes=[pltpu.VMEM((nblk, tq, LANES), I32)],
        ),
        out_shape=[jax.ShapeDtypeStruct((b, nblk, tq, LANES), F32), jax.ShapeDtypeStruct((b, tq, LANES), F32)],
        compiler_params=_params("parallel", "arbitrary"),
        name="sidx",
    )(page_table, iq2, w2, iknew_t, *([idx_pages_t] * pps))


def _sattn_kernel(pt_ref, qbd_ref, keys_ref, thr_ref, knew_ref, vnew_ref, *rest, ppb, n_pages):
    del pt_ref
    kpages, vpages = rest[:ppb], rest[ppb:2 * ppb]
    o_ref, m_scr, l_scr, acc_scr = rest[2 * ppb:]
    s = pl.program_id(1)
    nh = N_HEADS_A
    qbd = qbd_ref[0]
    tq = qbd.shape[0] // nh
    thr = jnp.concatenate([thr_ref[0, :, :1]] * nh, axis=0)

    @pl.when(s == 0)
    def _():
        m_scr[...] = jnp.full(m_scr.shape, NEG_BIG, F32)
        l_scr[...] = jnp.zeros(l_scr.shape, F32)
        acc_scr[...] = jnp.zeros(acc_scr.shape, F32)

    def pages_update(kts, vts, blks):
        kt = jnp.concatenate([a.astype(BF16) for a in kts], axis=1)
        vt = jnp.concatenate([a.astype(BF16) for a in vts], axis=1)
        kk = jnp.concatenate([keys_ref[0, blk] for blk in blks], axis=1)
        sc = jnp.dot(qbd, kt, preferred_element_type=F32)
        sc = jnp.where(jnp.concatenate([kk] * nh, axis=0) >= thr, sc, NEG_BIG)
        m_old = m_scr[:, :1]
        m_new = jnp.maximum(m_old, jnp.max(sc, axis=1, keepdims=True))
        alpha = jnp.exp2(m_old - m_new)
        p = jnp.exp2(sc - m_new)
        l_scr[...] = alpha * l_scr[...] + jnp.sum(p, axis=1, keepdims=True)
        pv = lax.dot_general(p.astype(BF16), vt, NT_DIMS, preferred_element_type=F32)
        acc_scr[...] = alpha * acc_scr[...] + pv
        m_scr[...] = jnp.broadcast_to(m_new, m_scr.shape)

    pages_update([r[...] for r in kpages], [r[...] for r in vpages], [s * ppb + j for j in range(ppb)])

    @pl.when(s == pl.num_programs(1) - 1)
    def _():
        pages_update([knew_ref[0]], [vnew_ref[0]], [n_pages])
        acc = acc_scr[...]
        o = jnp.concatenate([acc[h * tq:(h + 1) * tq, h * HEAD_DIM_A:(h + 1) * HEAD_DIM_A] for h in range(nh)],
                            axis=0)
        o_ref[0] = o / l_scr[:, :HEAD_DIM_A]


def _sample_attention(page_table, qbd, keys, thr, knew_t, vnew_t, k_pages_t, v_pages_t):
    b, n_pages = page_table.shape
    ppb = min(KV_PAGES_PER_STEP, n_pages)
    rows = qbd.shape[1]
    per_b3 = lambda r, w: pl.BlockSpec((1, r, w), lambda bi, s, pt: (bi, 0, 0))
    page = lambda j: pl.BlockSpec((None, WIDTH_A, PAGE_SIZE), lambda bi, s, pt, j=j: (pt[bi, s * ppb + j], 0, 0))
    return pl.pallas_call(
        functools.partial(_sattn_kernel, ppb=ppb, n_pages=n_pages),
        grid_spec=pltpu.PrefetchScalarGridSpec(
            num_scalar_prefetch=1,
            grid=(b, n_pages // ppb),
            in_specs=[per_b3(rows, WIDTH_A),
                      pl.BlockSpec((1,) + keys.shape[1:], lambda bi, s, pt: (bi, 0, 0, 0)),
                      per_b3(thr.shape[1], LANES), per_b3(WIDTH_A, PAGE_SIZE), per_b3(WIDTH_A, PAGE_SIZE)]
            + [page(j) for j in range(ppb)] * 2,
            out_specs=per_b3(rows, HEAD_DIM_A),
            scratch_shapes=[pltpu.VMEM((rows, LANES), F32), pltpu.VMEM((rows, LANES), F32),
                            pltpu.VMEM((rows, WIDTH_A), F32)],
        ),
        out_shape=jax.ShapeDtypeStruct((b, rows, HEAD_DIM_A), F32),
        compiler_params=_params("parallel", "arbitrary"),
        name="sattn",
    )(page_table, qbd, keys, thr, knew_t, vnew_t, *([k_pages_t] * ppb), *([v_pages_t] * ppb))


def _unmask_heads(xm, n_heads):
    x = xm.reshape(xm.shape[0], n_heads, LANES)
    return jnp.stack([x[:, h, (h % 2) * HEAD_DIM_A:(h % 2 + 1) * HEAD_DIM_A] for h in range(n_heads)], axis=1)


def _sample_layer(x_sample, cache_k, cache_v, cache_idx_k, state, page_table, w):
    b, t, d = x_sample.shape
    n = b * t
    n_pages = page_table.shape[1]
    past = n_pages * PAGE_SIZE
    pos = past + jnp.arange(t, dtype=I32)
    x1, p = _pre(x_sample.reshape(n, d), w, _rot_tables(jnp.tile(pos, b)), n, n, False)

    iq2 = _unmask_heads(p["iqm"], N_HEADS_IDX).reshape(b, t, N_HEADS_IDX, IDX_DIM)
    iq2 = iq2.transpose(0, 2, 1, 3).reshape(b, N_HEADS_IDX * t, IDX_DIM)
    w2 = p["ikw"][:, IDX_DIM:IDX_DIM + N_HEADS_IDX].reshape(b, t, N_HEADS_IDX).transpose(0, 2, 1)
    w2 = jnp.broadcast_to(w2.reshape(b, N_HEADS_IDX * t, 1), (b, N_HEADS_IDX * t, LANES))
    new_page = lambda a: jnp.pad(a.reshape(b, t, -1).transpose(0, 2, 1), ((0, 0), (0, 0), (0, PAGE_SIZE - t)))
    n_sel = min(TOPK_MAX, (past + t) // 4)
    keys, thr = _sample_index(page_table, iq2, w2, new_page(p["ik2"][:, :IDX_DIM]),
                              cache_idx_k[0].transpose(0, 2, 1), n_sel)

    q2 = _unmask_heads(p["qm"], N_HEADS_A).reshape(b, t, N_HEADS_A, HEAD_DIM_A).transpose(0, 2, 1, 3)
    eye = jnp.eye(N_HEADS_A, dtype=q2.dtype)
    qbd = (q2[:, :, :, None, :] * eye[None, :, None, :, None]).reshape(b, N_HEADS_A * t, WIDTH_A)
    pages_t = lambda c: c[0].transpose(0, 2, 3, 1).reshape(-1, WIDTH_A, PAGE_SIZE)
    o2 = _sample_attention(page_table, qbd, keys, thr, new_page(p["k"]), new_page(p["v"]),
                           pages_t(cache_k), pages_t(cache_v))
    attn = o2.reshape(b, N_HEADS_A, t, HEAD_DIM_A).transpose(0, 2, 1, 3).reshape(n, WIDTH_A)

    tp = -(-t // CHUNK_B) * CHUNK_B
    r3 = lambda a: jnp.pad(a.reshape(b, t, a.shape[-1]), ((0, 0), (0, tp - t), (0, 0)))
    o, s_new = _hgrn(r3(p["hq"]), r3(p["kf"]), r3(p["lf"]), r3(p["hi"]), state, tp)
    y = _post(x1, attn, o[:, :t].reshape(n, WIDTH_B), p, w, n, n)
    return (y.reshape(b, t, d),
            p["k"].reshape(b, t, N_HEADS_A, HEAD_DIM_A), p["v"].reshape(b, t, N_HEADS_A, HEAD_DIM_A),
            p["ikw"][:, :IDX_DIM].reshape(b, t, IDX_DIM), s_new)


def kernel(x_prompt, x_sample, cache_k, cache_v, cache_idx_k, state_hgrn, page_table,
           ffn1_norm, ffn1_w_gate, ffn1_w_up, ffn1_w_down, mix_norm, w_in, q_norm, k_norm,
           hgrn_lb, hgrn_o_norm, w_proj_attn, w_proj_hgrn, w_out,
           ffn2_norm, ffn2_w_gate, ffn2_w_up, ffn2_w_down):
    w = _prep_weights(ffn1_norm, ffn1_w_gate, ffn1_w_up, ffn1_w_down, mix_norm, w_in, q_norm, k_norm,
                      hgrn_lb, hgrn_o_norm, w_proj_attn, w_proj_hgrn, w_out,
                      ffn2_norm, ffn2_w_gate, ffn2_w_up, ffn2_w_down)
    yp, kp, vp, ikp, sp = _prompt_layer(x_prompt, w)
    ys, ks, vs, iks, ss = _sample_layer(x_sample, cache_k, cache_v, cache_idx_k, state_hgrn[0], page_table, w)
    return (yp, ys, kp[None], vp[None], ikp[None], sp[None], ks[None], vs[None], iks[None], ss[None])
```

```python
import functools

import numpy as np
import jax
import jax.numpy as jnp
from jax import lax
from jax.experimental import pallas as pl
from jax.experimental.pallas import tpu as pltpu

F32 = jnp.float32
BF16 = jnp.bfloat16
I32 = jnp.int32

D_MODEL = 1024
PAST_LEN = 16384
PAGE_SIZE = 128
N_HEADS_A = 8
HEAD_DIM_A = 64
ROT_DIM = HEAD_DIM_A // 4
ROPE_THETA = 500000.0
N_HEADS_IDX = 4
IDX_DIM = 64
IDX_W_SCALE = (N_HEADS_IDX * IDX_DIM) ** -0.5
TOPK_MAX = 256
N_HEADS_B = 4
KEY_DIM_B = 128
VAL_DIM_B = 128
CHUNK_B = 16
WIDTH_A = N_HEADS_A * HEAD_DIM_A
WIDTH_B = N_HEADS_B * VAL_DIM_B
KEYW_B = N_HEADS_B * KEY_DIM_B
EPS = 1e-6
IN_SIZES = (WIDTH_A, WIDTH_A, WIDTH_A, N_HEADS_IDX * IDX_DIM, IDX_DIM, N_HEADS_IDX,
            KEYW_B, KEYW_B, WIDTH_B, WIDTH_B, D_MODEL, D_MODEL)
IN_OFFSETS = tuple(int(o) for o in np.cumsum(IN_SIZES)[:-1])

LANES = 128
SUBLANES = 8
VMEM_LIMIT_BYTES = 56 * 1024 * 1024
INT_MIN = np.int32(-2 ** 31)
NEG_BIG = -1e30
FF_CHUNK = 256
IN_CHUNK = 512
NT_DIMS = (((1,), (1,)), ((), ()))
LOG2E = 1.4426950408889634
ATTN_TILE = 256


def _params(*sem):
    return pltpu.CompilerParams(dimension_semantics=sem, vmem_limit_bytes=VMEM_LIMIT_BYTES)


def _float_key(x):
    u = pltpu.bitcast(x + 0.0, I32)
    return u ^ ((u >> 31) & np.int32(0x7FFFFFFF))


def _ffn_kernel(x_ref, g_ref, wg_ref, wu_ref, wd_ref, o_ref, h_scr, acc_scr):
    x = x_ref[...]
    ms = jnp.mean(x * x, axis=-1, keepdims=True)
    h_scr[...] = (x * lax.rsqrt(ms + EPS) * g_ref[...]).astype(BF16)
    acc_scr[...] = jnp.zeros_like(acc_scr)

    def chunk(c, carry):
        h = h_scr[...]
        cs = pl.ds(pl.multiple_of(c * FF_CHUNK, FF_CHUNK), FF_CHUNK)
        g = jnp.dot(h, wg_ref[:, cs], preferred_element_type=F32)
        u = jnp.dot(h, wu_ref[:, cs], preferred_element_type=F32)
        a = (g * jax.nn.sigmoid(g) * u).astype(BF16)
        acc_scr[...] += jnp.dot(a, wd_ref[cs, :], preferred_element_type=F32)
        return carry

    lax.fori_loop(0, wg_ref.shape[1] // FF_CHUNK, chunk, 0)
    o_ref[...] = x_ref[...] + 0.5 * acc_scr[...]


def _resident(shape):
    return pl.BlockSpec(shape, lambda i: (0,) * len(shape), pipeline_mode=pl.Buffered(1))


def _ffn(x, gain, wg, wu, wd, tm):
    n, d = x.shape
    return pl.pallas_call(
        _ffn_kernel,
        grid=(n // tm,),
        in_specs=[pl.BlockSpec((tm, d), lambda i: (i, 0)), _resident((1, d)),
                  _resident(wg.shape), _resident(wu.shape), _resident(wd.shape)],
        out_specs=pl.BlockSpec((tm, d), lambda i: (i, 0)),
        out_shape=jax.ShapeDtypeStruct((n, d), F32),
        scratch_shapes=[pltpu.VMEM((tm, d), BF16), pltpu.VMEM((tm, d), F32)],
        compiler_params=_params("parallel"),
        name="ffn",
    )(x, gain.reshape(1, d), wg, wu, wd)


_J_Q, _J_K, _J_V, _J_IDX, _J_HQ, _J_HF, _J_HI, _J_HG, _J_GA0, _J_GA1, _J_GB0, _J_GB1 = range(12)


def _rot(xc, c, sa, sb):
    return (xc * c + pltpu.roll(xc, LANES - ROT_DIM // 2, 1) * sa
            + pltpu.roll(xc, ROT_DIM // 2, 1) * sb)


def _inproj_kernel(x_ref, g_ref, w_ref, bd_ref, qg_ref, kg_ref, lb_ref, cos_ref, sa_ref, sb_ref,
                   qm_ref, k_ref, kb_ref, v_ref, vt_ref, iqm_ref, ikw_ref, ik2_ref,
                   hq_ref, kf_ref, lf_ref, hi_ref, hg_ref, ga_ref, gb_ref, h_scr, *, kv_cols):
    tm = x_ref.shape[0]
    x = x_ref[...]
    ms = jnp.mean(x * x, axis=-1, keepdims=True)
    h_scr[...] = (x * lax.rsqrt(ms + EPS) * g_ref[...]).astype(BF16)

    def chunk(jj):
        def run(fn):
            fn(jnp.dot(h_scr[...], w_ref[jj], preferred_element_type=F32))
        return run

    lane = lax.broadcasted_iota(I32, (tm, LANES), 1)
    lo_half = lane < HEAD_DIM_A

    def head_norm(t, gain):
        ms = jnp.dot((t * t).astype(BF16), bd_ref[...], preferred_element_type=F32)
        return t * lax.rsqrt(ms + EPS) * gain

    @chunk(_J_Q)
    def _(y):
        yn = head_norm(y, qg_ref[...])
        c, sa, sb = cos_ref[...], sa_ref[...], sb_ref[...]
        for p in range(WIDTH_A // LANES):
            r = _rot(yn[:, p * LANES:(p + 1) * LANES], c, sa, sb) * (HEAD_DIM_A ** -0.5 * LOG2E)
            qm_ref[:, (2 * p) * LANES:(2 * p + 1) * LANES] = jnp.where(lo_half, r, 0.0).astype(BF16)
            qm_ref[:, (2 * p + 1) * LANES:(2 * p + 2) * LANES] = jnp.where(lo_half, 0.0, r).astype(BF16)

    @chunk(_J_K)
    def _(y):
        yn = head_norm(y, kg_ref[...])
        c, sa, sb = cos_ref[...], sa_ref[...], sb_ref[...]
        for p in range(WIDTH_A // LANES):
            sl = slice(p * LANES, (p + 1) * LANES)
            r = _rot(yn[:, sl], c, sa, sb)
            if kv_cols:
                k_ref[0, sl, :] = r.T
            else:
                k_ref[:, sl] = r
            kb_ref[:, sl] = r.astype(BF16)

    @chunk(_J_V)
    def _(y):
        yt = y.T
        if kv_cols:
            v_ref[0] = yt
        else:
            v_ref[...] = y
        for u in range(vt_ref.shape[0]):
            vt_ref[u] = yt[:, u * vt_ref.shape[2]:(u + 1) * vt_ref.shape[2]].astype(BF16)

    @chunk(_J_IDX)
    def _(y):
        c, sa, sb = cos_ref[...], sa_ref[...], sb_ref[...]
        for p in range(2):
            r = _rot(y[:, p * LANES:(p + 1) * LANES], c, sa, sb)
            iqm_ref[:, (2 * p) * LANES:(2 * p + 1) * LANES] = jnp.where(lo_half, r, 0.0).astype(BF16)
            iqm_ref[:, (2 * p + 1) * LANES:(2 * p + 2) * LANES] = jnp.where(lo_half, 0.0, r).astype(BF16)
        r = _rot(y[:, 2 * LANES:3 * LANES], jnp.where(lo_half, c, 1.0),
                 jnp.where(lo_half, sa, 0.0), jnp.where(lo_half, sb, 0.0))
        if kv_cols:
            ikw_ref[0] = r.T
        else:
            ikw_ref[...] = r
        ik2_ref[...] = jnp.where(lo_half, r, pltpu.roll(r, HEAD_DIM_A, 1)).astype(BF16)

    @chunk(_J_HQ)
    def _(y):
        hq_ref[...] = (y * jax.nn.sigmoid(y)).astype(BF16)

    @chunk(_J_HF)
    def _(y):
        lb = lb_ref[...]
        lf_ref[...] = jnp.log(lb + (1.0 - lb) * jax.nn.sigmoid(y))
        kf_ref[...] = ((1.0 - lb) * jax.nn.sigmoid(-y)).astype(BF16)

    @chunk(_J_HI)
    def _(y):
        hi_ref[...] = y.astype(BF16)

    @chunk(_J_HG)
    def _(y):
        hg_ref[...] = (y * jax.nn.sigmoid(y)).astype(BF16)

    for jj, ref, half in ((_J_GA0, ga_ref, 0), (_J_GA1, ga_ref, 1), (_J_GB0, gb_ref, 0), (_J_GB1, gb_ref, 1)):
        @chunk(jj)
        def _(y, ref=ref, half=half):
            ref[:, half * IN_CHUNK:(half + 1) * IN_CHUNK] = jax.nn.sigmoid(y).astype(BF16)


def _inproj(x, gain, w12, bd, qg, kg, lb, cos_t, sa_t, sb_t, tm, kv_cols):
    n, d = x.shape
    nt = cos_t.shape[0] // tm
    row = lambda w: pl.BlockSpec((tm, w), lambda i: (i, 0))
    tab = pl.BlockSpec((tm, LANES), lambda i: (i % nt, 0))
    vt_tile = min(ATTN_TILE, tm)

    def rows(w, dt):
        return row(w), jax.ShapeDtypeStruct((n, w), dt)

    def kv(w):
        if kv_cols:
            return (pl.BlockSpec((1, w, tm), lambda i: (i // nt, 0, i % nt)),
                    jax.ShapeDtypeStruct((n // (nt * tm), w, nt * tm), F32))
        return rows(w, F32)

    outs = [rows(2 * WIDTH_A, BF16), kv(WIDTH_A), rows(WIDTH_A, BF16), kv(WIDTH_A),
            (pl.BlockSpec((tm // vt_tile, WIDTH_A, vt_tile), lambda i: (i, 0, 0)),
             jax.ShapeDtypeStruct((n // vt_tile, WIDTH_A, vt_tile), BF16)),
            rows(2 * N_HEADS_IDX * IDX_DIM, BF16), kv(LANES), rows(LANES, BF16),
            rows(KEYW_B, BF16), rows(KEYW_B, BF16), rows(KEYW_B, F32), rows(WIDTH_B, BF16), rows(WIDTH_B, BF16),
            rows(D_MODEL, BF16), rows(D_MODEL, BF16)]
    return pl.pallas_call(
        functools.partial(_inproj_kernel, kv_cols=kv_cols),
        grid=(n // tm,),
        in_specs=[row(d), _resident((1, d)), _resident(w12.shape), _resident(bd.shape),
                  _resident(qg.shape), _resident(kg.shape), _resident(lb.shape), tab, tab, tab],
        out_specs=[o[0] for o in outs],
        out_shape=[o[1] for o in outs],
        scratch_shapes=[pltpu.VMEM((tm, d), BF16)],
        compiler_params=_params("parallel"),
        name="inproj",
    )(x, gain.reshape(1, d), w12, bd, qg, kg, lb, cos_t, sa_t, sb_t)


def _tree(parts, op):
    while len(parts) > 1:
        parts = [op(a, b) for a, b in zip(parts[0::2], parts[1::2])] + ([parts[-1]] if len(parts) % 2 else [])
    return parts[0]


def _tree_sum(parts):
    return _tree(parts, jnp.add)


def _refine_threshold(cand, n_sel, count_fn, next_above, next_below):
    neg_flt_max_key = np.int32(-2139095040)
    key = jnp.maximum(cand, neg_flt_max_key)
    t = pltpu.bitcast(key ^ ((key >> 31) & np.int32(0x7FFFFFFF)), F32)

    def state(t):
        cge, cgt = count_fn(t, False), count_fn(t, True)
        lower = jnp.where(cand == INT_MIN, 0.0, -1.0)
        return cge, cgt, jnp.where(cgt >= n_sel, 1.0, jnp.where(cge < n_sel, lower, 0.0))

    def body(st):
        t, _, _, move = st
        t = jnp.where(move > 0.0, next_above(t), jnp.where(move < 0.0, next_below(t), t))
        return (t,) + state(t)

    t, cge, cgt, _ = lax.while_loop(lambda st: jnp.max(jnp.abs(st[3])) > 0.0, body, (t,) + state(t))
    return t, cge, cgt


def _search_threshold(count_ge, shape, n_sel):
    c0 = count_ge(jnp.zeros(shape, I32))
    ok = c0 >= n_sel
    cand = jnp.where(ok, np.int32(0), INT_MIN)
    ccnt = jnp.where(ok, c0, 0.0)

    def bit_body(i, carry):
        cand, ccnt = carry
        t = cand + (jnp.int32(1) << (30 - i))
        c = count_ge(t)
        ok = c >= n_sel
        return jnp.where(ok, t, cand), jnp.where(ok, c, ccnt)

    return lax.fori_loop(0, 31, bit_body, (cand, ccnt))


def _pattn_kernel(iqm_ref, ikw_ref, ik2_ref, qm_ref, k_ref, vt_ref, o_ref,
                  keys_scr, sc_scr, m_scr, a_scr, acc_scr, s_scr, p_scr, *, n_sel, tq):
    qi = pl.program_id(1)
    nkb = qi + 1
    krow = lax.broadcasted_iota(I32, (tq, tq), 0)
    qcol = lax.broadcasted_iota(I32, (tq, tq), 1)
    half = HEAD_DIM_A

    ikw_t = ikw_ref[0]
    wrow = [ikw_t[IDX_DIM + h:IDX_DIM + h + 1, :] * IDX_W_SCALE for h in range(N_HEADS_IDX)]

    def score_blk(kb, c):
        ik2 = ik2_ref[0, pl.ds(pl.multiple_of(kb * tq, tq), tq), :]
        sc = jnp.zeros((tq, tq), F32)
        for h in range(N_HEADS_IDX):
            d = lax.dot_general(ik2, iqm_ref[0, :, h * LANES:(h + 1) * LANES], NT_DIMS,
                                preferred_element_type=F32)
            sc = sc + jnp.maximum(d, 0.0) * wrow[h]
        off = jnp.where(kb < qi, jnp.int32(tq), jnp.int32(0))
        adm = krow <= qcol + off
        keys_scr[kb] = jnp.where(adm, _float_key(sc), INT_MIN)
        sc_scr[kb] = jnp.where(adm, sc, -jnp.inf)
        return c

    lax.fori_loop(0, nkb, score_blk, 0)
    keys_scr[nkb] = jnp.full((tq, tq), INT_MIN, I32)
    sc_scr[nkb] = jnp.full((tq, tq), -jnp.inf, F32)

    def over_rows(tile_fn, op, init):
        def block(kb):
            return _tree([tile_fn(kb, slice(r * SUBLANES, (r + 1) * SUBLANES))
                          for r in range(tq // SUBLANES)], op)

        def body(pi, acc):
            return op(acc, op(block(2 * pi), block(2 * pi + 1)))
        return lax.fori_loop(0, (nkb + 1) // 2, body, jnp.full((SUBLANES, tq), init, F32))

    def count_ge(t):
        acc = over_rows(lambda kb, rs: jnp.where(keys_scr[kb, rs, :] >= t, 1.0, 0.0), jnp.add, 0.0)
        return jnp.sum(acc, axis=0, keepdims=True)

    cand, _ = _search_threshold(count_ge, (1, tq), n_sel)

    def fcount(t, strict):
        hit = (lambda s: s > t) if strict else (lambda s: s >= t)
        acc = over_rows(lambda kb, rs: jnp.where(hit(sc_scr[kb, rs, :]), 1.0, 0.0), jnp.add, 0.0)
        return jnp.sum(acc, axis=0, keepdims=True)

    def next_above(t):
        def tile(kb, rs):
            s = sc_scr[kb, rs, :]
            return jnp.where(s > t, s, jnp.inf)
        return jnp.min(over_rows(tile, jnp.minimum, jnp.inf), axis=0, keepdims=True)

    def next_below(t):
        def tile(kb, rs):
            s = sc_scr[kb, rs, :]
            return jnp.where(s < t, s, -jnp.inf)
        return jnp.max(over_rows(tile, jnp.maximum, -jnp.inf), axis=0, keepdims=True)

    thr, cge, cgt = _refine_threshold(cand, n_sel, fcount, next_above, next_below)

    @pl.when(jnp.max(cge) > n_sel)
    def _():
        need = n_sel - cgt
        lower = jnp.where(qcol < krow, 1.0, 0.0).astype(BF16)

        def tie_blk(kb, seen):
            s = sc_scr[kb]
            tf = jnp.where(s == thr, 1.0, 0.0)
            rank = jnp.dot(lower, tf.astype(BF16), preferred_element_type=F32) + seen
            drop = jnp.where(rank >= need, tf, 0.0)
            sc_scr[kb] = jnp.where(drop > 0.0, -jnp.inf, s)
            return seen + jnp.sum(tf, axis=0, keepdims=True)

        lax.fori_loop(0, nkb, tie_blk, jnp.zeros((1, tq), F32))

    m_scr[...] = jnp.full(m_scr.shape, NEG_BIG, F32)
    acc_scr[...] = jnp.zeros(acc_scr.shape, F32)
    ones_half = jnp.ones((half, tq), BF16)

    def attn_blk(kb, c):
        bias = jnp.where(sc_scr[kb] >= thr, 0.0, NEG_BIG)
        ks = pl.ds(pl.multiple_of(kb * tq, tq), tq)
        for h in range(N_HEADS_A):
            pr = h // 2
            s_scr[h] = lax.dot_general(k_ref[0, ks, pr * LANES:(pr + 1) * LANES],
                                       qm_ref[0, :, h * LANES:(h + 1) * LANES], NT_DIMS,
                                       preferred_element_type=F32) + bias
        for h in range(N_HEADS_A):
            s = s_scr[h]
            m_old = m_scr[h]
            m_new = jnp.maximum(m_old, jnp.max(s, axis=0, keepdims=True))
            a_scr[h] = jnp.exp2(m_old - m_new)
            p_scr[h] = jnp.exp2(s - m_new).astype(BF16)
            m_scr[h] = m_new
        for h in range(N_HEADS_A):
            pr = h // 2
            vt = vt_ref[0, kb, pr * LANES:(pr + 1) * LANES, :]
            lhs = (jnp.concatenate([vt[:half], ones_half], axis=0) if h % 2 == 0
                   else jnp.concatenate([ones_half, vt[half:]], axis=0))
            acc_scr[h] = a_scr[h] * acc_scr[h] + jnp.dot(lhs, p_scr[h], preferred_element_type=F32)
        return c

    lax.fori_loop(0, nkb, attn_blk, 0)
    for pr in range(N_HEADS_A // 2):
        ae, ao = acc_scr[2 * pr], acc_scr[2 * pr + 1]
        ot = jnp.concatenate([ae[:half] / ae[half:], ao[half:] / ao[:half]], axis=0)
        o_ref[0, :, pr * LANES:(pr + 1) * LANES] = ot.T


def _prompt_attention(iqm, ikw, ik2, qm, kb, vt, n_sel, tq):
    b, t, _ = qm.shape
    nq = t // tq
    qblk = lambda w: pl.BlockSpec((1, tq, w), lambda bi, qi: (bi, qi, 0))
    full = lambda w: pl.BlockSpec((1, t, w), lambda bi, qi: (bi, 0, 0))
    return pl.pallas_call(
        functools.partial(_pattn_kernel, n_sel=n_sel, tq=tq),
        grid=(b, nq),
        in_specs=[qblk(iqm.shape[2]), pl.BlockSpec((1, LANES, tq), lambda bi, qi: (bi, 0, qi)), full(LANES),
                  qblk(qm.shape[2]), full(WIDTH_A),
                  pl.BlockSpec((1, nq, WIDTH_A, tq), lambda bi, qi: (bi, 0, 0, 0))],
        out_specs=qblk(WIDTH_A),
        out_shape=jax.ShapeDtypeStruct((b, t, WIDTH_A), F32),
        scratch_shapes=[
            pltpu.VMEM((nq + 1, tq, tq), I32),
            pltpu.VMEM((nq + 1, tq, tq), F32),
            pltpu.VMEM((N_HEADS_A, 1, tq), F32),
            pltpu.VMEM((N_HEADS_A, 1, tq), F32),
            pltpu.VMEM((N_HEADS_A, LANES, tq), F32),
            pltpu.VMEM((N_HEADS_A, tq, tq), F32),
            pltpu.VMEM((N_HEADS_A, tq, tq), BF16),
        ],
        compiler_params=_params("parallel", "arbitrary"),
        name="pattn",
    )(iqm, ikw, ik2, qm, kb, vt)


def _hgrn_kernel(q_ref, k_ref, g_ref, v_ref, s0_ref, o_ref, sn_ref, s_scr, *, tc):
    ti = pl.program_id(1)
    c_ = CHUNK_B

    streams = [(bi, h) for bi in range(q_ref.shape[0]) for h in range(N_HEADS_B)]

    @pl.when(ti == 0)
    def _():
        for si, (bi, h) in enumerate(streams):
            s_scr[si] = s0_ref[bi, h].T

    rowi = lax.broadcasted_iota(I32, (c_, LANES), 0)
    lanei = lax.broadcasted_iota(I32, (c_, LANES), 1)

    def chunk(ci, carry):
        rs = pl.ds(pl.multiple_of(ci * c_, c_), c_)
        for si, (bi, h) in enumerate(streams):
            hs = slice(h * LANES, (h + 1) * LANES)
            q = q_ref[bi, rs, hs].astype(F32)
            k = k_ref[bi, rs, hs].astype(F32)
            v = v_ref[bi, rs, hs]
            b = g_ref[bi, rs, hs] * LOG2E
            for sh in (1, 2, 4, 8):
                b = b + jnp.where(rowi >= sh, pltpu.roll(b, sh, 0), 0.0)
            b_last = b[c_ - 1:c_, :]
            st_old = s_scr[si]
            o = lax.dot_general((q * jnp.exp2(b)).astype(BF16), st_old.astype(BF16), NT_DIMS,
                                preferred_element_type=F32)
            cols = []
            for t in range(c_):
                e = jnp.exp2(jnp.minimum(b[t:t + 1, :] - b, 0.0))
                col = jnp.sum(e * k * q[t:t + 1, :], axis=1, keepdims=True)
                cols.append(jnp.where(lanei == t, col, 0.0))
            att = jnp.where(rowi <= lanei, _tree_sum(cols), 0.0).T[:c_]
            o_ref[bi, rs, hs] = o + jnp.dot(att.astype(BF16), v, preferred_element_type=F32)
            kd = k * jnp.exp2(b_last - b)
            upd = jnp.dot(v.astype(F32).T.astype(BF16), kd.astype(BF16), preferred_element_type=F32)
            s_scr[si] = st_old * jnp.exp2(b_last) + upd
        return carry

    lax.fori_loop(0, tc // c_, chunk, 0)

    @pl.when(ti == pl.num_programs(1) - 1)
    def _():
        for si, (bi, h) in enumerate(streams):
            sn_ref[bi, h] = s_scr[si].T


HGRN_BATCH_GROUP = 4


def _hgrn(q, k, g, v, s0, tc):
    b, t, w = q.shape
    bg = min(HGRN_BATCH_GROUP, b)
    blk = pl.BlockSpec((bg, tc, w), lambda bi, ti: (bi, ti, 0))
    st = pl.BlockSpec((bg, N_HEADS_B, KEY_DIM_B, VAL_DIM_B), lambda bi, ti: (bi, 0, 0, 0))
    return pl.pallas_call(
        functools.partial(_hgrn_kernel, tc=tc),
        grid=(b // bg, t // tc),
        in_specs=[blk, blk, blk, blk, st],
        out_specs=[blk, st],
        out_shape=[jax.ShapeDtypeStruct((b, t, w), F32), jax.ShapeDtypeStruct(s0.shape, F32)],
        scratch_shapes=[pltpu.VMEM((bg * N_HEADS_B, KEY_DIM_B, VAL_DIM_B), F32)],
        compiler_params=_params("parallel", "arbitrary"),
        name="hgrn",
    )(q, k, g, v, s0)


def _merge_kernel(attn_ref, o_ref, hg_ref, ga_ref, gb_ref, x_ref, on_ref, wpa_ref, wph_ref, wo_ref, y_ref):
    o = o_ref[...]
    parts = []
    for h in range(N_HEADS_B):
        oh = o[:, h * LANES:(h + 1) * LANES]
        ms = jnp.mean(oh * oh, axis=-1, keepdims=True)
        parts.append(oh * lax.rsqrt(ms + EPS))
    on = jnp.concatenate(parts, axis=1) * on_ref[...] * hg_ref[...]
    pa = jnp.dot(attn_ref[...].astype(BF16), wpa_ref[...], preferred_element_type=F32)
    ph = jnp.dot(on.astype(BF16), wph_ref[...], preferred_element_type=F32)
    merged = ga_ref[...] * pa + gb_ref[...] * ph
    y_ref[...] = x_ref[...] + jnp.dot(merged.astype(BF16), wo_ref[...], preferred_element_type=F32)


def _merge(attn, o, hg, ga, gb, x, onorm, wpa, wph, wo, tm):
    n, d = x.shape
    row = lambda w: pl.BlockSpec((tm, w), lambda i: (i, 0))
    const = lambda r, w: pl.BlockSpec((r, w), lambda i: (0, 0))
    return pl.pallas_call(
        _merge_kernel,
        grid=(n // tm,),
        in_specs=[row(WIDTH_A), row(WIDTH_B), row(WIDTH_B), row(d), row(d), row(d),
                  const(1, WIDTH_B), const(WIDTH_A, d), const(WIDTH_B, d), const(d, d)],
        out_specs=row(d),
        out_shape=jax.ShapeDtypeStruct((n, d), F32),
        compiler_params=_params("parallel"),
        name="merge",
    )(attn, o, hg, ga, gb, x, onorm, wpa, wph, wo)


def _rot_tables(pos):
    r = pos.shape[0]
    inv = ROPE_THETA ** (-jnp.arange(0, ROT_DIM, 2, dtype=F32) / ROT_DIM)
    ang = pos.astype(F32)[:, None] * inv[None, :]
    cos, sin = jnp.cos(ang), jnp.sin(ang)
    half = ROT_DIM // 2
    rest = HEAD_DIM_A - ROT_DIM
    one, z8, zr = jnp.ones((r, rest), F32), jnp.zeros((r, half), F32), jnp.zeros((r, rest), F32)
    c64 = jnp.concatenate([cos, cos, one], axis=1)
    a64 = jnp.concatenate([-sin, z8, zr], axis=1)
    b64 = jnp.concatenate([z8, sin, zr], axis=1)
    return tuple(jnp.concatenate([t, t], axis=1) for t in (c64, a64, b64))


def _prep_weights(ffn1_norm, ffn1_w_gate, ffn1_w_up, ffn1_w_down, mix_norm, w_in, q_norm, k_norm,
                  hgrn_lb, hgrn_o_norm, w_proj_attn, w_proj_hgrn, w_out,
                  ffn2_norm, ffn2_w_gate, ffn2_w_up, ffn2_w_down):
    l = 0
    q, k, v, iq, ik, iw, hq, hf, hi, hg, ga, gb = jnp.split(w_in[l], IN_OFFSETS, axis=1)
    pad = jnp.zeros((D_MODEL, IN_CHUNK - iq.shape[1] - ik.shape[1] - iw.shape[1]), F32)
    idx = jnp.concatenate([iq, ik, iw, pad], axis=1)
    w12 = jnp.stack([q, k, v, idx, hq, hf, hi, hg,
                     ga[:, :IN_CHUNK], ga[:, IN_CHUNK:], gb[:, :IN_CHUNK], gb[:, IN_CHUNK:]]).astype(BF16)
    head = np.arange(WIDTH_A) // HEAD_DIM_A
    bd = jnp.asarray((head[:, None] == head[None, :]).astype(np.float32) / HEAD_DIM_A, BF16)
    lb = jnp.cumsum(jax.nn.softmax(hgrn_lb.astype(F32), axis=0), axis=0)[l].reshape(1, KEYW_B)
    cols = rows = lambda a: a.astype(BF16)
    return dict(
        ffn1=(ffn1_norm[l], cols(ffn1_w_gate[l]), cols(ffn1_w_up[l]), rows(ffn1_w_down[l])),
        ffn2=(ffn2_norm[l], cols(ffn2_w_gate[l]), cols(ffn2_w_up[l]), rows(ffn2_w_down[l])),
        mix_norm=mix_norm[l], w12=w12, bd=bd,
        qg=jnp.tile(q_norm[l], N_HEADS_A).reshape(1, WIDTH_A),
        kg=jnp.tile(k_norm[l], N_HEADS_A).reshape(1, WIDTH_A),
        lb=lb,
        onorm=jnp.tile(hgrn_o_norm[l], N_HEADS_B).reshape(1, WIDTH_B),
        wpa=w_proj_attn[l].astype(BF16), wph=w_proj_hgrn[l].astype(BF16), wo=w_out[l].astype(BF16),
    )


def _pre(x, w, tabs, tm_ffn, tm_in, kv_cols):
    x1 = _ffn(x, *w["ffn1"], tm_ffn)
    names = ("qm", "k", "kb", "v", "vt", "iqm", "ikw", "ik2", "hq", "kf", "lf", "hi", "hg", "ga", "gb")
    outs = _inproj(x1, w["mix_norm"], w["w12"], w["bd"], w["qg"], w["kg"], w["lb"], *tabs, tm_in, kv_cols)
    return x1, dict(zip(names, outs))


def _post(x1, attn, o, p, w, tm_merge, tm_ffn):
    x2 = _merge(attn, o, p["hg"], p["ga"], p["gb"], x1, w["onorm"], w["wpa"], w["wph"], w["wo"], tm_merge)
    return _ffn(x2, *w["ffn2"], tm_ffn)


def _prompt_layer(x_prompt, w):
    b, t, d = x_prompt.shape
    n = b * t
    tabs = _rot_tables(jnp.arange(t, dtype=I32))
    x1, p = _pre(x_prompt.reshape(n, d), w, tabs, min(1024, n), min(512, t), True)
    r3 = lambda a: a.reshape(b, t, a.shape[-1])
    n_sel = min(TOPK_MAX, t // 4)
    tq = min(ATTN_TILE, t)
    vt = p["vt"].reshape(b, t // tq, WIDTH_A, tq)
    attn = _prompt_attention(r3(p["iqm"]), p["ikw"], r3(p["ik2"]), r3(p["qm"]), r3(p["kb"]), vt, n_sel, tq)
    s0 = jnp.zeros((b, N_HEADS_B, KEY_DIM_B, VAL_DIM_B), F32)
    o, s_new = _hgrn(r3(p["hq"]), r3(p["kf"]), r3(p["lf"]), r3(p["hi"]), s0, min(256, t))
    y = _post(x1, attn.reshape(n, WIDTH_A), o.reshape(n, WIDTH_B), p, w, min(512, n), min(1024, n))
    heads = lambda a: a.reshape(b, N_HEADS_A, HEAD_DIM_A, t).transpose(0, 3, 1, 2)
    return (y.reshape(b, t, d), heads(p["k"]), heads(p["v"]),
            p["ikw"][:, :IDX_DIM, :].transpose(0, 2, 1), s_new)


KV_PAGES_PER_STEP = 16
IDX_PAGE_UNROLL = 32


def _sidx_kernel(pt_ref, iq2_ref, w2_ref, iknew_ref, idx_hbm, sc_ref, thr_ref, keys_scr, page_buf, sems,
                 *, n_pages, n_sel, unroll):
    bi = pl.program_id(0)
    slot = bi % 2
    nblk = sc_ref.shape[1]

    def page_copy(b_, p, sl):
        return pltpu.make_async_copy(idx_hbm.at[pt_ref[b_, p]], page_buf.at[sl, p], sems.at[sl])

    def start_pages(b_, sl):
        def body(p, c):
            page_copy(b_, p, sl).start()
            return c
        lax.fori_loop(0, n_pages, body, 0)

    @pl.when(bi == 0)
    def _():
        start_pages(0, 0)

    @pl.when(bi + 1 < pl.num_programs(0))
    def _():
        start_pages(bi + 1, 1 - slot)

    def wait_page(p, c):
        page_copy(bi, p, slot).wait()
        return c

    lax.fori_loop(0, n_pages, wait_page, 0)
    iq2 = iq2_ref[0]
    w2 = w2_ref[0] * IDX_W_SCALE
    tq = iq2.shape[0] // N_HEADS_IDX

    def block_scores(ik_t):
        d = jnp.dot(iq2, ik_t, preferred_element_type=F32)
        r = jnp.maximum(d, 0.0) * w2
        sc = r[0:tq]
        for h in range(1, N_HEADS_IDX):
            sc = sc + r[h * tq:(h + 1) * tq]
        return sc

    def score_pages(g, c):
        for j in range(unroll):
            p = g * unroll + j
            sc = block_scores(page_buf[slot, p].astype(BF16))
            sc_ref[0, p] = sc
            keys_scr[p] = _float_key(sc)
        return c

    lax.fori_loop(0, n_pages // unroll, score_pages, 0)

    def finish():
        lane = lax.broadcasted_iota(I32, (tq, LANES), 1)
        row = lax.broadcasted_iota(I32, (tq, LANES), 0)
        sc = block_scores(iknew_ref[0])
        sc_ref[0, n_pages] = jnp.where(lane <= row, sc, -jnp.inf)
        keys_scr[n_pages] = jnp.where(lane <= row, _float_key(sc), INT_MIN)
        for u in range(n_pages + 1, nblk):
            sc_ref[0, u] = jnp.full((tq, LANES), -jnp.inf, F32)
            keys_scr[u] = jnp.full((tq, LANES), INT_MIN, I32)

        def over_blocks(tile_fn, op):
            return _tree([tile_fn(u) for u in range(nblk)], op)

        def count_ge(t):
            return jnp.sum(over_blocks(lambda u: jnp.where(keys_scr[u] >= t, 1.0, 0.0), jnp.add),
                           axis=1, keepdims=True)

        def fcount(t, strict):
            hit = (lambda x: x > t) if strict else (lambda x: x >= t)
            return jnp.sum(over_blocks(lambda u: jnp.where(hit(sc_ref[0, u]), 1.0, 0.0), jnp.add),
                           axis=1, keepdims=True)

        def next_above(t):
            def tile(u):
                x = sc_ref[0, u]
                return jnp.where(x > t, x, jnp.inf)
            return jnp.min(over_blocks(tile, jnp.minimum), axis=1, keepdims=True)

        def next_below(t):
            def tile(u):
                x = sc_ref[0, u]
                return jnp.where(x < t, x, -jnp.inf)
            return jnp.max(over_blocks(tile, jnp.maximum), axis=1, keepdims=True)

        cand, _ = _search_threshold(count_ge, (tq, 1), n_sel)
        thr, cge, cgt = _refine_threshold(cand, n_sel, fcount, next_above, next_below)
        thr_ref[0] = jnp.broadcast_to(thr, (tq, LANES))

        @pl.when(jnp.max(cge) > n_sel)
        def _():
            need = n_sel - cgt
            r2 = lax.broadcasted_iota(I32, (LANES, LANES), 0)
            c2 = lax.broadcasted_iota(I32, (LANES, LANES), 1)
            upper = jnp.where(r2 < c2, 1.0, 0.0)

            def tie_blk(u, seen):
                x = sc_ref[0, u]
                tf = jnp.where(x == thr, 1.0, 0.0)
                rank = jnp.dot(tf, upper, preferred_element_type=F32) + seen
                drop = jnp.where(rank >= need, tf, 0.0)
                sc_ref[0, u] = jnp.where(drop > 0.0, -jnp.inf, x)
                return seen + jnp.sum(tf, axis=1, keepdims=True)

            lax.fori_loop(0, n_pages + 1, tie_blk, jnp.zeros((tq, 1), F32))

    finish()


def _sample_index(page_table, iq2, w2, iknew_t, idx_pages_t, n_sel):
    b, n_pages = page_table.shape
    tq = iq2.shape[1] // N_HEADS_IDX
    nblk = -(-(n_pages + 1) // SUBLANES) * SUBLANES
    per_b = lambda r, w: pl.BlockSpec((1, r, w), lambda bi, pt: (bi, 0, 0))
    return pl.pallas_call(
        functools.partial(_sidx_kernel, n_pages=n_pages, n_sel=n_sel, unroll=min(IDX_PAGE_UNROLL, n_pages)),
        grid_spec=pltpu.PrefetchScalarGridSpec(
            num_scalar_prefetch=1,
            grid=(b,),
            in_specs=[per_b(N_HEADS_IDX * tq, IDX_DIM), per_b(N_HEADS_IDX * tq, LANES), per_b(IDX_DIM, PAGE_SIZE),
                      pl.BlockSpec(memory_space=pl.ANY)],
            out_specs=[pl.BlockSpec((1, nblk, tq, LANES), lambda bi, pt: (bi, 0, 0, 0)),
                       pl.BlockSpec((1, tq, LANES), lambda bi, pt: (bi, 0, 0))],
            scratch_shapes=[pltpu.VMEM((nblk, tq, LANES), I32),
                            pltpu.VMEM((2, n_pages, IDX_DIM, PAGE_SIZE), F32),
                            pltpu.SemaphoreType.DMA((2,))],
        ),
        out_shape=[jax.ShapeDtypeStruct((b, nblk, tq, LANES), F32), jax.ShapeDtypeStruct((b, tq, LANES), F32)],
        compiler_params=_params("arbitrary"),
        name="sidx",
    )(page_table, iq2, w2, iknew_t, idx_pages_t)


def _sattn_kernel(pt_ref, qbd_ref, keys_ref, thr_ref, knew_ref, vnew_ref, *rest, ppb, n_pages):
    del pt_ref
    kpages, vpages = rest[:ppb], rest[ppb:2 * ppb]
    o_ref, m_scr, l_scr, acc_scr = rest[2 * ppb:]
    s = pl.program_id(1)
    nh = N_HEADS_A
    qbd = qbd_ref[0]
    tq = qbd.shape[0] // nh
    thr = jnp.concatenate([thr_ref[0, :, :1]] * nh, axis=0)

    @pl.when(s == 0)
    def _():
        m_scr[...] = jnp.full(m_scr.shape, NEG_BIG, F32)
        l_scr[...] = jnp.zeros(l_scr.shape, F32)
        acc_scr[...] = jnp.zeros(acc_scr.shape, F32)

    def pages_update(kts, vts, blks):
        kt = jnp.concatenate([a.astype(BF16) for a in kts], axis=1)
        vt = jnp.concatenate([a.astype(BF16) for a in vts], axis=1)
        kk = jnp.concatenate([keys_ref[0, blk] for blk in blks], axis=1)
        sc = jnp.dot(qbd, kt, preferred_element_type=F32)
        sc = jnp.where(jnp.concatenate([kk] * nh, axis=0) >= thr, sc, NEG_BIG)
        m_old = m_scr[:, :1]
        m_new = jnp.maximum(m_old, jnp.max(sc, axis=1, keepdims=True))
        alpha = jnp.exp2(m_old - m_new)
        p = jnp.exp2(sc - m_new)
        l_scr[...] = alpha * l_scr[...] + jnp.sum(p, axis=1, keepdims=True)
        pv = lax.dot_general(p.astype(BF16), vt, NT_DIMS, preferred_element_type=F32)
        acc_scr[...] = alpha * acc_scr[...] + pv
        m_scr[...] = jnp.broadcast_to(m_new, m_scr.shape)

    pages_update([r[...] for r in kpages], [r[...] for r in vpages], [s * ppb + j for j in range(ppb)])

    @pl.when(s == pl.num_programs(1) - 1)
    def _():
        pages_update([knew_ref[0]], [vnew_ref[0]], [n_pages])
        acc = acc_scr[...]
        o = jnp.concatenate([acc[h * tq:(h + 1) * tq, h * HEAD_DIM_A:(h + 1) * HEAD_DIM_A] for h in range(nh)],
                            axis=0)
        o_ref[0] = o / l_scr[:, :HEAD_DIM_A]


def _sample_attention(page_table, qbd, keys, thr, knew_t, vnew_t, k_pages_t, v_pages_t):
    b, n_pages = page_table.shape
    ppb = min(KV_PAGES_PER_STEP, n_pages)
    rows = qbd.shape[1]
    per_b3 = lambda r, w: pl.BlockSpec((1, r, w), lambda bi, s, pt: (bi, 0, 0))
    page = lambda j: pl.BlockSpec((None, WIDTH_A, PAGE_SIZE), lambda bi, s, pt, j=j: (pt[bi, s * ppb + j], 0, 0))
    return pl.pallas_call(
        functools.partial(_sattn_kernel, ppb=ppb, n_pages=n_pages),
        grid_spec=pltpu.PrefetchScalarGridSpec(
            num_scalar_prefetch=1,
            grid=(b, n_pages // ppb),
            in_specs=[per_b3(rows, WIDTH_A),
                      pl.BlockSpec((1,) + keys.shape[1:], lambda bi, s, pt: (bi, 0, 0, 0)),
                      per_b3(thr.shape[1], LANES), per_b3(WIDTH_A, PAGE_SIZE), per_b3(WIDTH_A, PAGE_SIZE)]
            + [page(j) for j in range(ppb)] * 2,
            out_specs=per_b3(rows, HEAD_DIM_A),
            scratch_shapes=[pltpu.VMEM((rows, LANES), F32), pltpu.VMEM((rows, LANES), F32),
                            pltpu.VMEM((rows, WIDTH_A), F32)],
        ),
        out_shape=jax.ShapeDtypeStruct((b, rows, HEAD_DIM_A), F32),
        compiler_params=_params("parallel", "arbitrary"),
        name="sattn",
    )(page_table, qbd, keys, thr, knew_t, vnew_t, *([k_pages_t] * ppb), *([v_pages_t] * ppb))


def _unmask_heads(xm, n_heads):
    x = xm.reshape(xm.shape[0], n_heads, LANES)
    return jnp.stack([x[:, h, (h % 2) * HEAD_DIM_A:(h % 2 + 1) * HEAD_DIM_A] for h in range(n_heads)], axis=1)


def _sample_layer(x_sample, cache_k, cache_v, cache_idx_k, state, page_table, w):
    b, t, d = x_sample.shape
    n = b * t
    n_pages = page_table.shape[1]
    past = n_pages * PAGE_SIZE
    pos = past + jnp.arange(t, dtype=I32)
    x1, p = _pre(x_sample.reshape(n, d), w, _rot_tables(jnp.tile(pos, b)), n, n, False)

    iq2 = _unmask_heads(p["iqm"], N_HEADS_IDX).reshape(b, t, N_HEADS_IDX, IDX_DIM)
    iq2 = iq2.transpose(0, 2, 1, 3).reshape(b, N_HEADS_IDX * t, IDX_DIM)
    w2 = p["ikw"][:, IDX_DIM:IDX_DIM + N_HEADS_IDX].reshape(b, t, N_HEADS_IDX).transpose(0, 2, 1)
    w2 = jnp.broadcast_to(w2.reshape(b, N_HEADS_IDX * t, 1), (b, N_HEADS_IDX * t, LANES))
    new_page = lambda a: jnp.pad(a.reshape(b, t, -1).transpose(0, 2, 1), ((0, 0), (0, 0), (0, PAGE_SIZE - t)))
    n_sel = min(TOPK_MAX, (past + t) // 4)
    keys, thr = _sample_index(page_table, iq2, w2, new_page(p["ik2"][:, :IDX_DIM]),
                              cache_idx_k[0].transpose(0, 2, 1), n_sel)

    q2 = _unmask_heads(p["qm"], N_HEADS_A).reshape(b, t, N_HEADS_A, HEAD_DIM_A).transpose(0, 2, 1, 3)
    eye = jnp.eye(N_HEADS_A, dtype=q2.dtype)
    qbd = (q2[:, :, :, None, :] * eye[None, :, None, :, None]).reshape(b, N_HEADS_A * t, WIDTH_A)
    pages_t = lambda c: c[0].transpose(0, 2, 3, 1).reshape(-1, WIDTH_A, PAGE_SIZE)
    o2 = _sample_attention(page_table, qbd, keys, thr, new_page(p["k"]), new_page(p["v"]),
                           pages_t(cache_k), pages_t(cache_v))
    attn = o2.reshape(b, N_HEADS_A, t, HEAD_DIM_A).transpose(0, 2, 1, 3).reshape(n, WIDTH_A)

    tp = -(-t // CHUNK_B) * CHUNK_B
    r3 = lambda a: jnp.pad(a.reshape(b, t, a.shape[-1]), ((0, 0), (0, tp - t), (0, 0)))
    o, s_new = _hgrn(r3(p["hq"]), r3(p["kf"]), r3(p["lf"]), r3(p["hi"]), state, tp)
    y = _post(x1, attn, o[:, :t].reshape(n, WIDTH_B), p, w, n, n)
    return (y.reshape(b, t, d),
            p["k"].reshape(b, t, N_HEADS_A, HEAD_DIM_A), p["v"].reshape(b, t, N_HEADS_A, HEAD_DIM_A),
            p["ikw"][:, :IDX_DIM].reshape(b, t, IDX_DIM), s_new)


def kernel(x_prompt, x_sample, cache_k, cache_v, cache_idx_k, state_hgrn, page_table,
           ffn1_norm, ffn1_w_gate, ffn1_w_up, ffn1_w_down, mix_norm, w_in, q_norm, k_norm,
           hgrn_lb, hgrn_o_norm, w_proj_attn, w_proj_hgrn, w_out,
           ffn2_norm, ffn2_w_gate, ffn2_w_up, ffn2_w_down):
    w = _prep_weights(ffn1_norm, ffn1_w_gate, ffn1_w_up, ffn1_w_down, mix_norm, w_in, q_norm, k_norm,
                      hgrn_lb, hgrn_o_norm, w_proj_attn, w_proj_hgrn, w_out,
                      ffn2_norm, ffn2_w_gate, ffn2_w_up, ffn2_w_down)
    yp, kp, vp, ikp, sp = _prompt_layer(x_prompt, w)
    ys, ks, vs, iks, ss = _sample_layer(x_sample, cache_k, cache_v, cache_idx_k, state_hgrn[0], page_table, w)
    return (yp, ys, kp[None], vp[None], ikp[None], sp[None], ks[None], vs[None], iks[None], ss[None])
```

```python
import functools

import numpy as np
import jax
import jax.numpy as jnp
from jax import lax
from jax.experimental import pallas as pl
from jax.experimental.pallas import tpu as pltpu

F32 = jnp.float32
BF16 = jnp.bfloat16
I32 = jnp.int32

D_MODEL = 1024
PAST_LEN = 16384
PAGE_SIZE = 128
N_HEADS_A = 8
HEAD_DIM_A = 64
ROT_DIM = HEAD_DIM_A // 4
ROPE_THETA = 500000.0
N_HEADS_IDX = 4
IDX_DIM = 64
IDX_W_SCALE = (N_HEADS_IDX * IDX_DIM) ** -0.5
TOPK_MAX = 256
N_HEADS_B = 4
KEY_DIM_B = 128
VAL_DIM_B = 128
CHUNK_B = 16
WIDTH_A = N_HEADS_A * HEAD_DIM_A
WIDTH_B = N_HEADS_B * VAL_DIM_B
KEYW_B = N_HEADS_B * KEY_DIM_B
EPS = 1e-6
IN_SIZES = (WIDTH_A, WIDTH_A, WIDTH_A, N_HEADS_IDX * IDX_DIM, IDX_DIM, N_HEADS_IDX,
            KEYW_B, KEYW_B, WIDTH_B, WIDTH_B, D_MODEL, D_MODEL)
IN_OFFSETS = tuple(int(o) for o in np.cumsum(IN_SIZES)[:-1])

LANES = 128
SUBLANES = 8
VMEM_LIMIT_BYTES = 56 * 1024 * 1024
INT_MIN = np.int32(-2 ** 31)
NEG_BIG = -1e30
FF_CHUNK = 256
IN_CHUNK = 512
NT_DIMS = (((1,), (1,)), ((), ()))
LOG2E = 1.4426950408889634
ATTN_TILE = 256


def _params(*sem):
    return pltpu.CompilerParams(dimension_semantics=sem, vmem_limit_bytes=VMEM_LIMIT_BYTES)


def _float_key(x):
    u = pltpu.bitcast(x + 0.0, I32)
    return u ^ ((u >> 31) & np.int32(0x7FFFFFFF))


def _ffn_kernel(x_ref, g_ref, wg_ref, wu_ref, wd_ref, o_ref, h_scr, acc_scr):
    x = x_ref[...]
    ms = jnp.mean(x * x, axis=-1, keepdims=True)
    h_scr[...] = (x * lax.rsqrt(ms + EPS) * g_ref[...]).astype(BF16)
    acc_scr[...] = jnp.zeros_like(acc_scr)

    def chunk(c, carry):
        h = h_scr[...]
        cs = pl.ds(pl.multiple_of(c * FF_CHUNK, FF_CHUNK), FF_CHUNK)
        g = jnp.dot(h, wg_ref[:, cs], preferred_element_type=F32)
        u = jnp.dot(h, wu_ref[:, cs], preferred_element_type=F32)
        a = (g * jax.nn.sigmoid(g) * u).astype(BF16)
        acc_scr[...] += jnp.dot(a, wd_ref[cs, :], preferred_element_type=F32)
        return carry

    lax.fori_loop(0, wg_ref.shape[1] // FF_CHUNK, chunk, 0)
    o_ref[...] = x_ref[...] + 0.5 * acc_scr[...]


def _resident(shape):
    return pl.BlockSpec(shape, lambda i: (0,) * len(shape), pipeline_mode=pl.Buffered(1))


def _ffn(x, gain, wg, wu, wd, tm):
    n, d = x.shape
    return pl.pallas_call(
        _ffn_kernel,
        grid=(n // tm,),
        in_specs=[pl.BlockSpec((tm, d), lambda i: (i, 0)), _resident((1, d)),
                  _resident(wg.shape), _resident(wu.shape), _resident(wd.shape)],
        out_specs=pl.BlockSpec((tm, d), lambda i: (i, 0)),
        out_shape=jax.ShapeDtypeStruct((n, d), F32),
        scratch_shapes=[pltpu.VMEM((tm, d), BF16), pltpu.VMEM((tm, d), F32)],
        compiler_params=_params("parallel"),
        name="ffn",
    )(x, gain.reshape(1, d), wg, wu, wd)


_J_Q, _J_K, _J_V, _J_IDX, _J_HQ, _J_HF, _J_HI, _J_HG, _J_GA0, _J_GA1, _J_GB0, _J_GB1 = range(12)


def _rot(xc, c, sa, sb):
    return (xc * c + pltpu.roll(xc, LANES - ROT_DIM // 2, 1) * sa
            + pltpu.roll(xc, ROT_DIM // 2, 1) * sb)


def _inproj_kernel(x_ref, g_ref, w_ref, bd_ref, qg_ref, kg_ref, lb_ref, cos_ref, sa_ref, sb_ref,
                   qm_ref, k_ref, kb_ref, v_ref, vt_ref, iqm_ref, ikw_ref, ik2_ref,
                   hq_ref, kf_ref, lf_ref, hi_ref, hg_ref, ga_ref, gb_ref, h_scr, *, kv_cols):
    tm = x_ref.shape[0]
    x = x_ref[...]
    ms = jnp.mean(x * x, axis=-1, keepdims=True)
    h_scr[...] = (x * lax.rsqrt(ms + EPS) * g_ref[...]).astype(BF16)

    def chunk(jj):
        def run(fn):
            fn(jnp.dot(h_scr[...], w_ref[jj], preferred_element_type=F32))
        return run

    lane = lax.broadcasted_iota(I32, (tm, LANES), 1)
    lo_half = lane < HEAD_DIM_A

    def head_norm(t, gain):
        ms = jnp.dot((t * t).astype(BF16), bd_ref[...], preferred_element_type=F32)
        return t * lax.rsqrt(ms + EPS) * gain

    @chunk(_J_Q)
    def _(y):
        yn = head_norm(y, qg_ref[...])
        c, sa, sb = cos_ref[...], sa_ref[...], sb_ref[...]
        for p in range(WIDTH_A // LANES):
            r = _rot(yn[:, p * LANES:(p + 1) * LANES], c, sa, sb) * (HEAD_DIM_A ** -0.5 * LOG2E)
            qm_ref[:, (2 * p) * LANES:(2 * p + 1) * LANES] = jnp.where(lo_half, r, 0.0).astype(BF16)
            qm_ref[:, (2 * p + 1) * LANES:(2 * p + 2) * LANES] = jnp.where(lo_half, 0.0, r).astype(BF16)

    @chunk(_J_K)
    def _(y):
        yn = head_norm(y, kg_ref[...])
        c, sa, sb = cos_ref[...], sa_ref[...], sb_ref[...]
        for p in range(WIDTH_A // LANES):
            sl = slice(p * LANES, (p + 1) * LANES)
            r = _rot(yn[:, sl], c, sa, sb)
            if kv_cols:
                k_ref[0, sl, :] = r.T
            else:
                k_ref[:, sl] = r
            kb_ref[:, sl] = r.astype(BF16)

    @chunk(_J_V)
    def _(y):
        yt = y.T
        if kv_cols:
            v_ref[0] = yt
        else:
            v_ref[...] = y
        for u in range(vt_ref.shape[0]):
            vt_ref[u] = yt[:, u * vt_ref.shape[2]:(u + 1) * vt_ref.shape[2]].astype(BF16)

    @chunk(_J_IDX)
    def _(y):
        c, sa, sb = cos_ref[...], sa_ref[...], sb_ref[...]
        for p in range(2):
            r = _rot(y[:, p * LANES:(p + 1) * LANES], c, sa, sb)
            iqm_ref[:, (2 * p) * LANES:(2 * p + 1) * LANES] = jnp.where(lo_half, r, 0.0).astype(BF16)
            iqm_ref[:, (2 * p + 1) * LANES:(2 * p + 2) * LANES] = jnp.where(lo_half, 0.0, r).astype(BF16)
        r = _rot(y[:, 2 * LANES:3 * LANES], jnp.where(lo_half, c, 1.0),
                 jnp.where(lo_half, sa, 0.0), jnp.where(lo_half, sb, 0.0))
        if kv_cols:
            ikw_ref[0] = r.T
        else:
            ikw_ref[...] = r
        ik2_ref[...] = jnp.where(lo_half, r, pltpu.roll(r, HEAD_DIM_A, 1)).astype(BF16)

    @chunk(_J_HQ)
    def _(y):
        hq_ref[...] = (y * jax.nn.sigmoid(y)).astype(BF16)

    @chunk(_J_HF)
    def _(y):
        lb = lb_ref[...]
        lf_ref[...] = jnp.log(lb + (1.0 - lb) * jax.nn.sigmoid(y))
        kf_ref[...] = ((1.0 - lb) * jax.nn.sigmoid(-y)).astype(BF16)

    @chunk(_J_HI)
    def _(y):
        hi_ref[...] = y.astype(BF16)

    @chunk(_J_HG)
    def _(y):
        hg_ref[...] = (y * jax.nn.sigmoid(y)).astype(BF16)

    for jj, ref, half in ((_J_GA0, ga_ref, 0), (_J_GA1, ga_ref, 1), (_J_GB0, gb_ref, 0), (_J_GB1, gb_ref, 1)):
        @chunk(jj)
        def _(y, ref=ref, half=half):
            ref[:, half * IN_CHUNK:(half + 1) * IN_CHUNK] = jax.nn.sigmoid(y).astype(BF16)


def _inproj(x, gain, w12, bd, qg, kg, lb, cos_t, sa_t, sb_t, tm, kv_cols):
    n, d = x.shape
    nt = cos_t.shape[0] // tm
    row = lambda w: pl.BlockSpec((tm, w), lambda i: (i, 0))
    tab = pl.BlockSpec((tm, LANES), lambda i: (i % nt, 0))
    vt_tile = min(ATTN_TILE, tm)

    def rows(w, dt):
        return row(w), jax.ShapeDtypeStruct((n, w), dt)

    def kv(w):
        if kv_cols:
            return (pl.BlockSpec((1, w, tm), lambda i: (i // nt, 0, i % nt)),
                    jax.ShapeDtypeStruct((n // (nt * tm), w, nt * tm), F32))
        return rows(w, F32)

    outs = [rows(2 * WIDTH_A, BF16), kv(WIDTH_A), rows(WIDTH_A, BF16), kv(WIDTH_A),
            (pl.BlockSpec((tm // vt_tile, WIDTH_A, vt_tile), lambda i: (i, 0, 0)),
             jax.ShapeDtypeStruct((n // vt_tile, WIDTH_A, vt_tile), BF16)),
            rows(2 * N_HEADS_IDX * IDX_DIM, BF16), kv(LANES), rows(LANES, BF16),
            rows(KEYW_B, BF16), rows(KEYW_B, BF16), rows(KEYW_B, F32), rows(WIDTH_B, BF16), rows(WIDTH_B, BF16),
            rows(D_MODEL, BF16), rows(D_MODEL, BF16)]
    return pl.pallas_call(
        functools.partial(_inproj_kernel, kv_cols=kv_cols),
        grid=(n // tm,),
        in_specs=[row(d), _resident((1, d)), _resident(w12.shape), _resident(bd.shape),
                  _resident(qg.shape), _resident(kg.shape), _resident(lb.shape), tab, tab, tab],
        out_specs=[o[0] for o in outs],
        out_shape=[o[1] for o in outs],
        scratch_shapes=[pltpu.VMEM((tm, d), BF16)],
        compiler_params=_params("parallel"),
        name="inproj",
    )(x, gain.reshape(1, d), w12, bd, qg, kg, lb, cos_t, sa_t, sb_t)


def _tree(parts, op):
    while len(parts) > 1:
        parts = [op(a, b) for a, b in zip(parts[0::2], parts[1::2])] + ([parts[-1]] if len(parts) % 2 else [])
    return parts[0]


def _tree_sum(parts):
    return _tree(parts, jnp.add)


def _refine_threshold(cand, n_sel, count_fn, next_above, next_below):
    neg_flt_max_key = np.int32(-2139095040)
    key = jnp.maximum(cand, neg_flt_max_key)
    t = pltpu.bitcast(key ^ ((key >> 31) & np.int32(0x7FFFFFFF)), F32)

    def state(t):
        cge, cgt = count_fn(t, False), count_fn(t, True)
        lower = jnp.where(cand == INT_MIN, 0.0, -1.0)
        return cge, cgt, jnp.where(cgt >= n_sel, 1.0, jnp.where(cge < n_sel, lower, 0.0))

    def body(st):
        t, _, _, move = st
        t = jnp.where(move > 0.0, next_above(t), jnp.where(move < 0.0, next_below(t), t))
        return (t,) + state(t)

    t, cge, cgt, _ = lax.while_loop(lambda st: jnp.max(jnp.abs(st[3])) > 0.0, body, (t,) + state(t))
    return t, cge, cgt


def _search_threshold(count_ge, shape, n_sel):
    c0 = count_ge(jnp.zeros(shape, I32))
    ok = c0 >= n_sel
    cand = jnp.where(ok, np.int32(0), INT_MIN)
    ccnt = jnp.where(ok, c0, 0.0)

    def bit_body(i, carry):
        cand, ccnt = carry
        t = cand + (jnp.int32(1) << (30 - i))
        c = count_ge(t)
        ok = c >= n_sel
        return jnp.where(ok, t, cand), jnp.where(ok, c, ccnt)

    return lax.fori_loop(0, 31, bit_body, (cand, ccnt))


def _pattn_kernel(iqm_ref, ikw_ref, ik2_ref, qm_ref, k_ref, vt_ref, o_ref,
                  keys_scr, sc_scr, m_scr, a_scr, acc_scr, s_scr, p_scr, *, n_sel, tq):
    qi = pl.program_id(1)
    nkb = qi + 1
    krow = lax.broadcasted_iota(I32, (tq, tq), 0)
    qcol = lax.broadcasted_iota(I32, (tq, tq), 1)
    half = HEAD_DIM_A

    ikw_t = ikw_ref[0]
    wrow = [ikw_t[IDX_DIM + h:IDX_DIM + h + 1, :] * IDX_W_SCALE for h in range(N_HEADS_IDX)]

    def score_blk(kb, c):
        ik2 = ik2_ref[0, pl.ds(pl.multiple_of(kb * tq, tq), tq), :]
        sc = jnp.zeros((tq, tq), F32)
        for h in range(N_HEADS_IDX):
            d = lax.dot_general(ik2, iqm_ref[0, :, h * LANES:(h + 1) * LANES], NT_DIMS,
                                preferred_element_type=F32)
            sc = sc + jnp.maximum(d, 0.0) * wrow[h]
        off = jnp.where(kb < qi, jnp.int32(tq), jnp.int32(0))
        adm = krow <= qcol + off
        keys_scr[kb] = jnp.where(adm, _float_key(sc), INT_MIN)
        sc_scr[kb] = jnp.where(adm, sc, -jnp.inf)
        return c

    lax.fori_loop(0, nkb, score_blk, 0)
    keys_scr[nkb] = jnp.full((tq, tq), INT_MIN, I32)
    sc_scr[nkb] = jnp.full((tq, tq), -jnp.inf, F32)

    def over_rows(tile_fn, op, init):
        def block(kb):
            return _tree([tile_fn(kb, slice(r * SUBLANES, (r + 1) * SUBLANES))
                          for r in range(tq // SUBLANES)], op)

        def body(pi, acc):
            return op(acc, op(block(2 * pi), block(2 * pi + 1)))
        return lax.fori_loop(0, (nkb + 1) // 2, body, jnp.full((SUBLANES, tq), init, F32))

    def count_ge(t):
        acc = over_rows(lambda kb, rs: jnp.where(keys_scr[kb, rs, :] >= t, 1.0, 0.0), jnp.add, 0.0)
        return jnp.sum(acc, axis=0, keepdims=True)

    cand, _ = _search_threshold(count_ge, (1, tq), n_sel)

    def fcount(t, strict):
        hit = (lambda s: s > t) if strict else (lambda s: s >= t)
        acc = over_rows(lambda kb, rs: jnp.where(hit(sc_scr[kb, rs, :]), 1.0, 0.0), jnp.add, 0.0)
        return jnp.sum(acc, axis=0, keepdims=True)

    def next_above(t):
        def tile(kb, rs):
            s = sc_scr[kb, rs, :]
            return jnp.where(s > t, s, jnp.inf)
        return jnp.min(over_rows(tile, jnp.minimum, jnp.inf), axis=0, keepdims=True)

    def next_below(t):
        def tile(kb, rs):
            s = sc_scr[kb, rs, :]
            return jnp.where(s < t, s, -jnp.inf)
        return jnp.max(over_rows(tile, jnp.maximum, -jnp.inf), axis=0, keepdims=True)

    thr, cge, cgt = _refine_threshold(cand, n_sel, fcount, next_above, next_below)

    @pl.when(jnp.max(cge) > n_sel)
    def _():
        need = n_sel - cgt
        lower = jnp.where(qcol < krow, 1.0, 0.0).astype(BF16)

        def tie_blk(kb, seen):
            s = sc_scr[kb]
            tf = jnp.where(s == thr, 1.0, 0.0)
            rank = jnp.dot(lower, tf.astype(BF16), preferred_element_type=F32) + seen
            drop = jnp.where(rank >= need, tf, 0.0)
            sc_scr[kb] = jnp.where(drop > 0.0, -jnp.inf, s)
            return seen + jnp.sum(tf, axis=0, keepdims=True)

        lax.fori_loop(0, nkb, tie_blk, jnp.zeros((1, tq), F32))

    m_scr[...] = jnp.full(m_scr.shape, NEG_BIG, F32)
    acc_scr[...] = jnp.zeros(acc_scr.shape, F32)
    ones_half = jnp.ones((half, tq), BF16)

    def attn_blk(kb, c):
        bias = jnp.where(sc_scr[kb] >= thr, 0.0, NEG_BIG)
        ks = pl.ds(pl.multiple_of(kb * tq, tq), tq)
        for h in range(N_HEADS_A):
            pr = h // 2
            s_scr[h] = lax.dot_general(k_ref[0, ks, pr * LANES:(pr + 1) * LANES],
                                       qm_ref[0, :, h * LANES:(h + 1) * LANES], NT_DIMS,
                                       preferred_element_type=F32) + bias
        for h in range(N_HEADS_A):
            s = s_scr[h]
            m_old = m_scr[h]
            m_new = jnp.maximum(m_old, jnp.max(s, axis=0, keepdims=True))
            a_scr[h] = jnp.exp2(m_old - m_new)
            p_scr[h] = jnp.exp2(s - m_new).astype(BF16)
            m_scr[h] = m_new
        for h in range(N_HEADS_A):
            pr = h // 2
            vt = vt_ref[0, kb, pr * LANES:(pr + 1) * LANES, :]
            lhs = (jnp.concatenate([vt[:half], ones_half], axis=0) if h % 2 == 0
                   else jnp.concatenate([ones_half, vt[half:]], axis=0))
            acc_scr[h] = a_scr[h] * acc_scr[h] + jnp.dot(lhs, p_scr[h], preferred_element_type=F32)
        return c

    lax.fori_loop(0, nkb, attn_blk, 0)
    for pr in range(N_HEADS_A // 2):
        ae, ao = acc_scr[2 * pr], acc_scr[2 * pr + 1]
        ot = jnp.concatenate([ae[:half] / ae[half:], ao[half:] / ao[:half]], axis=0)
        o_ref[0, :, pr * LANES:(pr + 1) * LANES] = ot.T


def _prompt_attention(iqm, ikw, ik2, qm, kb, vt, n_sel, tq):
    b, t, _ = qm.shape
    nq = t // tq
    qblk = lambda w: pl.BlockSpec((1, tq, w), lambda bi, qi: (bi, qi, 0))
    full = lambda w: pl.BlockSpec((1, t, w), lambda bi, qi: (bi, 0, 0))
    return pl.pallas_call(
        functools.partial(_pattn_kernel, n_sel=n_sel, tq=tq),
        grid=(b, nq),
        in_specs=[qblk(iqm.shape[2]), pl.BlockSpec((1, LANES, tq), lambda bi, qi: (bi, 0, qi)), full(LANES),
                  qblk(qm.shape[2]), full(WIDTH_A),
                  pl.BlockSpec((1, nq, WIDTH_A, tq), lambda bi, qi: (bi, 0, 0, 0))],
        out_specs=qblk(WIDTH_A),
        out_shape=jax.ShapeDtypeStruct((b, t, WIDTH_A), F32),
        scratch_shapes=[
            pltpu.VMEM((nq + 1, tq, tq), I32),
            pltpu.VMEM((nq + 1, tq, tq), F32),
            pltpu.VMEM((N_HEADS_A, 1, tq), F32),
            pltpu.VMEM((N_HEADS_A, 1, tq), F32),
            pltpu.VMEM((N_HEADS_A, LANES, tq), F32),
            pltpu.VMEM((N_HEADS_A, tq, tq), F32),
            pltpu.VMEM((N_HEADS_A, tq, tq), BF16),
        ],
        compiler_params=_params("parallel", "arbitrary"),
        name="pattn",
    )(iqm, ikw, ik2, qm, kb, vt)


def _hgrn_kernel(q_ref, k_ref, g_ref, v_ref, s0_ref, o_ref, sn_ref, s_scr, *, tc):
    ti = pl.program_id(1)
    c_ = CHUNK_B

    streams = [(bi, h) for bi in range(q_ref.shape[0]) for h in range(N_HEADS_B)]

    @pl.when(ti == 0)
    def _():
        for si, (bi, h) in enumerate(streams):
            s_scr[si] = s0_ref[bi, h].T

    rowi = lax.broadcasted_iota(I32, (c_, LANES), 0)
    lanei = lax.broadcasted_iota(I32, (c_, LANES), 1)

    def chunk(ci, carry):
        rs = pl.ds(pl.multiple_of(ci * c_, c_), c_)
        for si, (bi, h) in enumerate(streams):
            hs = slice(h * LANES, (h + 1) * LANES)
            q = q_ref[bi, rs, hs].astype(F32)
            k = k_ref[bi, rs, hs].astype(F32)
            v = v_ref[bi, rs, hs]
            b = g_ref[bi, rs, hs] * LOG2E
            for sh in (1, 2, 4, 8):
                b = b + jnp.where(rowi >= sh, pltpu.roll(b, sh, 0), 0.0)
            b_last = b[c_ - 1:c_, :]
            st_old = s_scr[si]
            o = lax.dot_general((q * jnp.exp2(b)).astype(BF16), st_old.astype(BF16), NT_DIMS,
                                preferred_element_type=F32)
            cols = []
            for t in range(c_):
                e = jnp.exp2(jnp.minimum(b[t:t + 1, :] - b, 0.0))
                col = jnp.sum(e * k * q[t:t + 1, :], axis=1, keepdims=True)
                cols.append(jnp.where(lanei == t, col, 0.0))
            att = jnp.where(rowi <= lanei, _tree_sum(cols), 0.0).T[:c_]
            o_ref[bi, rs, hs] = o + jnp.dot(att.astype(BF16), v, preferred_element_type=F32)
            kd = k * jnp.exp2(b_last - b)
            upd = jnp.dot(v.astype(F32).T.astype(BF16), kd.astype(BF16), preferred_element_type=F32)
            s_scr[si] = st_old * jnp.exp2(b_last) + upd
        return carry

    lax.fori_loop(0, tc // c_, chunk, 0)

    @pl.when(ti == pl.num_programs(1) - 1)
    def _():
        for si, (bi, h) in enumerate(streams):
            sn_ref[bi, h] = s_scr[si].T


HGRN_BATCH_GROUP = 4


def _hgrn(q, k, g, v, s0, tc):
    b, t, w = q.shape
    bg = min(HGRN_BATCH_GROUP, b)
    blk = pl.BlockSpec((bg, tc, w), lambda bi, ti: (bi, ti, 0))
    st = pl.BlockSpec((bg, N_HEADS_B, KEY_DIM_B, VAL_DIM_B), lambda bi, ti: (bi, 0, 0, 0))
    return pl.pallas_call(
        functools.partial(_hgrn_kernel, tc=tc),
        grid=(b // bg, t // tc),
        in_specs=[blk, blk, blk, blk, st],
        out_specs=[blk, st],
        out_shape=[jax.ShapeDtypeStruct((b, t, w), F32), jax.ShapeDtypeStruct(s0.shape, F32)],
        scratch_shapes=[pltpu.VMEM((bg * N_HEADS_B, KEY_DIM_B, VAL_DIM_B), F32)],
        compiler_params=_params("parallel", "arbitrary"),
        name="hgrn",
    )(q, k, g, v, s0)


def _merge_kernel(attn_ref, o_ref, hg_ref, ga_ref, gb_ref, x_ref, on_ref, wpa_ref, wph_ref, wo_ref, y_ref):
    o = o_ref[...]
    parts = []
    for h in range(N_HEADS_B):
        oh = o[:, h * LANES:(h + 1) * LANES]
        ms = jnp.mean(oh * oh, axis=-1, keepdims=True)
        parts.append(oh * lax.rsqrt(ms + EPS))
    on = jnp.concatenate(parts, axis=1) * on_ref[...] * hg_ref[...]
    pa = jnp.dot(attn_ref[...].astype(BF16), wpa_ref[...], preferred_element_type=F32)
    ph = jnp.dot(on.astype(BF16), wph_ref[...], preferred_element_type=F32)
    merged = ga_ref[...] * pa + gb_ref[...] * ph
    y_ref[...] = x_ref[...] + jnp.dot(merged.astype(BF16), wo_ref[...], preferred_element_type=F32)


def _merge(attn, o, hg, ga, gb, x, onorm, wpa, wph, wo, tm):
    n, d = x.shape
    row = lambda w: pl.BlockSpec((tm, w), lambda i: (i, 0))
    const = lambda r, w: pl.BlockSpec((r, w), lambda i: (0, 0))
    return pl.pallas_call(
        _merge_kernel,
        grid=(n // tm,),
        in_specs=[row(WIDTH_A), row(WIDTH_B), row(WIDTH_B), row(d), row(d), row(d),
                  const(1, WIDTH_B), const(WIDTH_A, d), const(WIDTH_B, d), const(d, d)],
        out_specs=row(d),
        out_shape=jax.ShapeDtypeStruct((n, d), F32),
        compiler_params=_params("parallel"),
        name="merge",
    )(attn, o, hg, ga, gb, x, onorm, wpa, wph, wo)


def _rot_tables(pos):
    r = pos.shape[0]
    inv = ROPE_THETA ** (-jnp.arange(0, ROT_DIM, 2, dtype=F32) / ROT_DIM)
    ang = pos.astype(F32)[:, None] * inv[None, :]
    cos, sin = jnp.cos(ang), jnp.sin(ang)
    half = ROT_DIM // 2
    rest = HEAD_DIM_A - ROT_DIM
    one, z8, zr = jnp.ones((r, rest), F32), jnp.zeros((r, half), F32), jnp.zeros((r, rest), F32)
    c64 = jnp.concatenate([cos, cos, one], axis=1)
    a64 = jnp.concatenate([-sin, z8, zr], axis=1)
    b64 = jnp.concatenate([z8, sin, zr], axis=1)
    return tuple(jnp.concatenate([t, t], axis=1) for t in (c64, a64, b64))


def _prep_weights(ffn1_norm, ffn1_w_gate, ffn1_w_up, ffn1_w_down, mix_norm, w_in, q_norm, k_norm,
                  hgrn_lb, hgrn_o_norm, w_proj_attn, w_proj_hgrn, w_out,
                  ffn2_norm, ffn2_w_gate, ffn2_w_up, ffn2_w_down):
    l = 0
    q, k, v, iq, ik, iw, hq, hf, hi, hg, ga, gb = jnp.split(w_in[l], IN_OFFSETS, axis=1)
    pad = jnp.zeros((D_MODEL, IN_CHUNK - iq.shape[1] - ik.shape[1] - iw.shape[1]), F32)
    idx = jnp.concatenate([iq, ik, iw, pad], axis=1)
    w12 = jnp.stack([q, k, v, idx, hq, hf, hi, hg,
                     ga[:, :IN_CHUNK], ga[:, IN_CHUNK:], gb[:, :IN_CHUNK], gb[:, IN_CHUNK:]]).astype(BF16)
    head = np.arange(WIDTH_A) // HEAD_DIM_A
    bd = jnp.asarray((head[:, None] == head[None, :]).astype(np.float32) / HEAD_DIM_A, BF16)
    lb = jnp.cumsum(jax.nn.softmax(hgrn_lb.astype(F32), axis=0), axis=0)[l].reshape(1, KEYW_B)
    cols = rows = lambda a: a.astype(BF16)
    return dict(
        ffn1=(ffn1_norm[l], cols(ffn1_w_gate[l]), cols(ffn1_w_up[l]), rows(ffn1_w_down[l])),
        ffn2=(ffn2_norm[l], cols(ffn2_w_gate[l]), cols(ffn2_w_up[l]), rows(ffn2_w_down[l])),
        mix_norm=mix_norm[l], w12=w12, bd=bd,
        qg=jnp.tile(q_norm[l], N_HEADS_A).reshape(1, WIDTH_A),
        kg=jnp.tile(k_norm[l], N_HEADS_A).reshape(1, WIDTH_A),
        lb=lb,
        onorm=jnp.tile(hgrn_o_norm[l], N_HEADS_B).reshape(1, WIDTH_B),
        wpa=w_proj_attn[l].astype(BF16), wph=w_proj_hgrn[l].astype(BF16), wo=w_out[l].astype(BF16),
    )


def _pre(x, w, tabs, tm_ffn, tm_in, kv_cols):
    x1 = _ffn(x, *w["ffn1"], tm_ffn)
    names = ("qm", "k", "kb", "v", "vt", "iqm", "ikw", "ik2", "hq", "kf", "lf", "hi", "hg", "ga", "gb")
    outs = _inproj(x1, w["mix_norm"], w["w12"], w["bd"], w["qg"], w["kg"], w["lb"], *tabs, tm_in, kv_cols)
    return x1, dict(zip(names, outs))


def _post(x1, attn, o, p, w, tm_merge, tm_ffn):
    x2 = _merge(attn, o, p["hg"], p["ga"], p["gb"], x1, w["onorm"], w["wpa"], w["wph"], w["wo"], tm_merge)
    return _ffn(x2, *w["ffn2"], tm_ffn)


def _prompt_layer(x_prompt, w):
    b, t, d = x_prompt.shape
    n = b * t
    tabs = _rot_tables(jnp.arange(t, dtype=I32))
    x1, p = _pre(x_prompt.reshape(n, d), w, tabs, min(1024, n), min(512, t), True)
    r3 = lambda a: a.reshape(b, t, a.shape[-1])
    n_sel = min(TOPK_MAX, t // 4)
    tq = min(ATTN_TILE, t)
    vt = p["vt"].reshape(b, t // tq, WIDTH_A, tq)
    attn = _prompt_attention(r3(p["iqm"]), p["ikw"], r3(p["ik2"]), r3(p["qm"]), r3(p["kb"]), vt, n_sel, tq)
    s0 = jnp.zeros((b, N_HEADS_B, KEY_DIM_B, VAL_DIM_B), F32)
    o, s_new = _hgrn(r3(p["hq"]), r3(p["kf"]), r3(p["lf"]), r3(p["hi"]), s0, min(256, t))
    y = _post(x1, attn.reshape(n, WIDTH_A), o.reshape(n, WIDTH_B), p, w, min(512, n), min(1024, n))
    heads = lambda a: a.reshape(b, N_HEADS_A, HEAD_DIM_A, t).transpose(0, 3, 1, 2)
    return (y.reshape(b, t, d), heads(p["k"]), heads(p["v"]),
            p["ikw"][:, :IDX_DIM, :].transpose(0, 2, 1), s_new)


KV_PAGES_PER_STEP = 16
IDX_PAGE_UNROLL = 32


def _sidx_kernel(pt_ref, iq2_ref, w2_ref, iknew_ref, idx_hbm, sc_ref, thr_ref, keys_scr, page_buf, sems,
                 *, n_pages, n_sel, unroll):
    bi = pl.program_id(0)
    slot = bi % 2
    nblk = sc_ref.shape[1]

    def page_copy(b_, p, sl):
        return pltpu.make_async_copy(idx_hbm.at[pt_ref[b_, p]], page_buf.at[sl, p], sems.at[sl])

    def start_pages(b_, sl):
        def body(p, c):
            page_copy(b_, p, sl).start()
            return c
        lax.fori_loop(0, n_pages, body, 0)

    @pl.when(bi == 0)
    def _():
        start_pages(0, 0)

    @pl.when(bi + 1 < pl.num_programs(0))
    def _():
        start_pages(bi + 1, 1 - slot)

    def wait_page(p, c):
        page_copy(bi, p, slot).wait()
        return c

    lax.fori_loop(0, n_pages, wait_page, 0)
    iq2 = iq2_ref[0]
    w2 = w2_ref[0] * IDX_W_SCALE
    tq = iq2.shape[0] // N_HEADS_IDX

    def block_scores(ik_t):
        d = jnp.dot(iq2, ik_t, preferred_element_type=F32)
        r = jnp.maximum(d, 0.0) * w2
        sc = r[0:tq]
        for h in range(1, N_HEADS_IDX):
            sc = sc + r[h * tq:(h + 1) * tq]
        return sc

    def score_pages(g, c):
        for j in range(unroll):
            p = g * unroll + j
            sc = block_scores(page_buf[slot, p].astype(BF16))
            sc_ref[0, p] = sc
            keys_scr[p] = _float_key(sc)
        return c

    lax.fori_loop(0, n_pages // unroll, score_pages, 0)

    def finish():
        lane = lax.broadcasted_iota(I32, (tq, LANES), 1)
        row = lax.broadcasted_iota(I32, (tq, LANES), 0)
        sc = block_scores(iknew_ref[0])
        sc_ref[0, n_pages] = jnp.where(lane <= row, sc, -jnp.inf)
        keys_scr[n_pages] = jnp.where(lane <= row, _float_key(sc), INT_MIN)
        for u in range(n_pages + 1, nblk):
            sc_ref[0, u] = jnp.full((tq, LANES), -jnp.inf, F32)
            keys_scr[u] = jnp.full((tq, LANES), INT_MIN, I32)

        def over_blocks(tile_fn, op):
            accs = [tile_fn(u) for u in range(SUBLANES)]
            for u in range(SUBLANES, nblk):
                accs[u % SUBLANES] = op(accs[u % SUBLANES], tile_fn(u))
            return _tree(accs, op)

        def count_ge(t):
            return jnp.sum(over_blocks(lambda u: jnp.where(keys_scr[u] >= t, 1.0, 0.0), jnp.add),
                           axis=1, keepdims=True)

        def fcount(t, strict):
            hit = (lambda x: x > t) if strict else (lambda x: x >= t)
            return jnp.sum(over_blocks(lambda u: jnp.where(hit(sc_ref[0, u]), 1.0, 0.0), jnp.add),
                           axis=1, keepdims=True)

        def next_above(t):
            def tile(u):
                x = sc_ref[0, u]
                return jnp.where(x > t, x, jnp.inf)
            return jnp.min(over_blocks(tile, jnp.minimum), axis=1, keepdims=True)

        def next_below(t):
            def tile(u):
                x = sc_ref[0, u]
                return jnp.where(x < t, x, -jnp.inf)
            return jnp.max(over_blocks(tile, jnp.maximum), axis=1, keepdims=True)

        cand, _ = _search_threshold(count_ge, (tq, 1), n_sel)
        thr, cge, cgt = _refine_threshold(cand, n_sel, fcount, next_above, next_below)
        thr_ref[0] = jnp.broadcast_to(thr, (tq, LANES))

        @pl.when(jnp.max(cge) > n_sel)
        def _():
            need = n_sel - cgt
            r2 = lax.broadcasted_iota(I32, (LANES, LANES), 0)
            c2 = lax.broadcasted_iota(I32, (LANES, LANES), 1)
            upper = jnp.where(r2 < c2, 1.0, 0.0)

            def tie_blk(u, seen):
                x = sc_ref[0, u]
                tf = jnp.where(x == thr, 1.0, 0.0)
                rank = jnp.dot(tf, upper, preferred_element_type=F32) + seen
                drop = jnp.where(rank >= need, tf, 0.0)
                sc_ref[0, u] = jnp.where(drop > 0.0, -jnp.inf, x)
                return seen + jnp.sum(tf, axis=1, keepdims=True)

            lax.fori_loop(0, n_pages + 1, tie_blk, jnp.zeros((tq, 1), F32))

    finish()


def _sample_index(page_table, iq2, w2, iknew_t, idx_pages_t, n_sel):
    b, n_pages = page_table.shape
    tq = iq2.shape[1] // N_HEADS_IDX
    nblk = -(-(n_pages + 1) // SUBLANES) * SUBLANES
    per_b = lambda r, w: pl.BlockSpec((1, r, w), lambda bi, pt: (bi, 0, 0))
    return pl.pallas_call(
        functools.partial(_sidx_kernel, n_pages=n_pages, n_sel=n_sel, unroll=min(IDX_PAGE_UNROLL, n_pages)),
        grid_spec=pltpu.PrefetchScalarGridSpec(
            num_scalar_prefetch=1,
            grid=(b,),
            in_specs=[per_b(N_HEADS_IDX * tq, IDX_DIM), per_b(N_HEADS_IDX * tq, LANES), per_b(IDX_DIM, PAGE_SIZE),
                      pl.BlockSpec(memory_space=pl.ANY)],
            out_specs=[pl.BlockSpec((1, nblk, tq, LANES), lambda bi, pt: (bi, 0, 0, 0)),
                       pl.BlockSpec((1, tq, LANES), lambda bi, pt: (bi, 0, 0))],
            scratch_shapes=[pltpu.VMEM((nblk, tq, LANES), I32),
                            pltpu.VMEM((2, n_pages, IDX_DIM, PAGE_SIZE), F32),
                            pltpu.SemaphoreType.DMA((2,))],
        ),
        out_shape=[jax.ShapeDtypeStruct((b, nblk, tq, LANES), F32), jax.ShapeDtypeStruct((b, tq, LANES), F32)],
        compiler_params=_params("arbitrary"),
        name="sidx",
    )(page_table, iq2, w2, iknew_t, idx_pages_t)


def _sattn_kernel(pt_ref, qbd_ref, keys_ref, thr_ref, knew_ref, vnew_ref, *rest, ppb, n_pages):
    del pt_ref
    kpages, vpages = rest[:ppb], rest[ppb:2 * ppb]
    o_ref, m_scr, l_scr, acc_scr = rest[2 * ppb:]
    s = pl.program_id(1)
    nh = N_HEADS_A
    qbd = qbd_ref[0]
    tq = qbd.shape[0] // nh
    thr = jnp.concatenate([thr_ref[0, :, :1]] * nh, axis=0)

    @pl.when(s == 0)
    def _():
        m_scr[...] = jnp.full(m_scr.shape, NEG_BIG, F32)
        l_scr[...] = jnp.zeros(l_scr.shape, F32)
        acc_scr[...] = jnp.zeros(acc_scr.shape, F32)

    def pages_update(kts, vts, blks):
        kt = jnp.concatenate([a.astype(BF16) for a in kts], axis=1)
        vt = jnp.concatenate([a.astype(BF16) for a in vts], axis=1)
        kk = jnp.concatenate([keys_ref[0, blk] for blk in blks], axis=1)
        sc = jnp.dot(qbd, kt, preferred_element_type=F32)
        sc = jnp.where(jnp.concatenate([kk] * nh, axis=0) >= thr, sc, NEG_BIG)
        m_old = m_scr[:, :1]
        m_new = jnp.maximum(m_old, jnp.max(sc, axis=1, keepdims=True))
        alpha = jnp.exp2(m_old - m_new)
        p = jnp.exp2(sc - m_new)
        l_scr[...] = alpha * l_scr[...] + jnp.sum(p, axis=1, keepdims=True)
        pv = lax.dot_general(p.astype(BF16), vt, NT_DIMS, preferred_element_type=F32)
        acc_scr[...] = alpha * acc_scr[...] + pv
        m_scr[...] = jnp.broadcast_to(m_new, m_scr.shape)

    pages_update([r[...] for r in kpages], [r[...] for r in vpages], [s * ppb + j for j in range(ppb)])

    @pl.when(s == pl.num_programs(1) - 1)
    def _():
        pages_update([knew_ref[0]], [vnew_ref[0]], [n_pages])
        acc = acc_scr[...]
        o = jnp.concatenate([acc[h * tq:(h + 1) * tq, h * HEAD_DIM_A:(h + 1) * HEAD_DIM_A] for h in range(nh)],
                            axis=0)
        o_ref[0] = o / l_scr[:, :HEAD_DIM_A]


def _sample_attention(page_table, qbd, keys, thr, knew_t, vnew_t, k_pages_t, v_pages_t):
    b, n_pages = page_table.shape
    ppb = min(KV_PAGES_PER_STEP, n_pages)
    rows = qbd.shape[1]
    per_b3 = lambda r, w: pl.BlockSpec((1, r, w), lambda bi, s, pt: (bi, 0, 0))
    page = lambda j: pl.BlockSpec((None, WIDTH_A, PAGE_SIZE), lambda bi, s, pt, j=j: (pt[bi, s * ppb + j], 0, 0))
    return pl.pallas_call(
        functools.partial(_sattn_kernel, ppb=ppb, n_pages=n_pages),
        grid_spec=pltpu.PrefetchScalarGridSpec(
            num_scalar_prefetch=1,
            grid=(b, n_pages // ppb),
            in_specs=[per_b3(rows, WIDTH_A),
                      pl.BlockSpec((1,) + keys.shape[1:], lambda bi, s, pt: (bi, 0, 0, 0)),
                      per_b3(thr.shape[1], LANES), per_b3(WIDTH_A, PAGE_SIZE), per_b3(WIDTH_A, PAGE_SIZE)]
            + [page(j) for j in range(ppb)] * 2,
            out_specs=per_b3(rows, HEAD_DIM_A),
            scratch_shapes=[pltpu.VMEM((rows, LANES), F32), pltpu.VMEM((rows, LANES), F32),
                            pltpu.VMEM((rows, WIDTH_A), F32)],
        ),
        out_shape=jax.ShapeDtypeStruct((b, rows, HEAD_DIM_A), F32),
        compiler_params=_params("parallel", "arbitrary"),
        name="sattn",
    )(page_table, qbd, keys, thr, knew_t, vnew_t, *([k_pages_t] * ppb), *([v_pages_t] * ppb))


def _unmask_heads(xm, n_heads):
    x = xm.reshape(xm.shape[0], n_heads, LANES)
    return jnp.stack([x[:, h, (h % 2) * HEAD_DIM_A:(h % 2 + 1) * HEAD_DIM_A] for h in range(n_heads)], axis=1)


def _sample_layer(x_sample, cache_k, cache_v, cache_idx_k, state, page_table, w):
    b, t, d = x_sample.shape
    n = b * t
    n_pages = page_table.shape[1]
    past = n_pages * PAGE_SIZE
    pos = past + jnp.arange(t, dtype=I32)
    x1, p = _pre(x_sample.reshape(n, d), w, _rot_tables(jnp.tile(pos, b)), n, n, False)

    iq2 = _unmask_heads(p["iqm"], N_HEADS_IDX).reshape(b, t, N_HEADS_IDX, IDX_DIM)
    iq2 = iq2.transpose(0, 2, 1, 3).reshape(b, N_HEADS_IDX * t, IDX_DIM)
    w2 = p["ikw"][:, IDX_DIM:IDX_DIM + N_HEADS_IDX].reshape(b, t, N_HEADS_IDX).transpose(0, 2, 1)
    w2 = jnp.broadcast_to(w2.reshape(b, N_HEADS_IDX * t, 1), (b, N_HEADS_IDX * t, LANES))
    new_page = lambda a: jnp.pad(a.reshape(b, t, -1).transpose(0, 2, 1), ((0, 0), (0, 0), (0, PAGE_SIZE - t)))
    n_sel = min(TOPK_MAX, (past + t) // 4)
    keys, thr = _sample_index(page_table, iq2, w2, new_page(p["ik2"][:, :IDX_DIM]),
                              cache_idx_k[0].transpose(0, 2, 1), n_sel)

    q2 = _unmask_heads(p["qm"], N_HEADS_A).reshape(b, t, N_HEADS_A, HEAD_DIM_A).transpose(0, 2, 1, 3)
    eye = jnp.eye(N_HEADS_A, dtype=q2.dtype)
    qbd = (q2[:, :, :, None, :] * eye[None, :, None, :, None]).reshape(b, N_HEADS_A * t, WIDTH_A)
    pages_t = lambda c: c[0].transpose(0, 2, 3, 1).reshape(-1, WIDTH_A, PAGE_SIZE)
    o2 = _sample_attention(page_table, qbd, keys, thr, new_page(p["k"]), new_page(p["v"]),
                           pages_t(cache_k), pages_t(cache_v))
    attn = o2.reshape(b, N_HEADS_A, t, HEAD_DIM_A).transpose(0, 2, 1, 3).reshape(n, WIDTH_A)

    tp = -(-t // CHUNK_B) * CHUNK_B
    r3 = lambda a: jnp.pad(a.reshape(b, t, a.shape[-1]), ((0, 0), (0, tp - t), (0, 0)))
    o, s_new = _hgrn(r3(p["hq"]), r3(p["kf"]), r3(p["lf"]), r3(p["hi"]), state, tp)
    y = _post(x1, attn, o[:, :t].reshape(n, WIDTH_B), p, w, n, n)
    return (y.reshape(b, t, d),
            p["k"].reshape(b, t, N_HEADS_A, HEAD_DIM_A), p["v"].reshape(b, t, N_HEADS_A, HEAD_DIM_A),
            p["ikw"][:, :IDX_DIM].reshape(b, t, IDX_DIM), s_new)


def kernel(x_prompt, x_sample, cache_k, cache_v, cache_idx_k, state_hgrn, page_table,
           ffn1_norm, ffn1_w_gate, ffn1_w_up, ffn1_w_down, mix_norm, w_in, q_norm, k_norm,
           hgrn_lb, hgrn_o_norm, w_proj_attn, w_proj_hgrn, w_out,
           ffn2_norm, ffn2_w_gate, ffn2_w_up, ffn2_w_down):
    w = _prep_weights(ffn1_norm, ffn1_w_gate, ffn1_w_up, ffn1_w_down, mix_norm, w_in, q_norm, k_norm,
                      hgrn_lb, hgrn_o_norm, w_proj_attn, w_proj_hgrn, w_out,
                      ffn2_norm, ffn2_w_gate, ffn2_w_up, ffn2_w_down)
    yp, kp, vp, ikp, sp = _prompt_layer(x_prompt, w)
    ys, ks, vs, iks, ss = _sample_layer(x_sample, cache_k, cache_v, cache_idx_k, state_hgrn[0], page_table, w)
    return (yp, ys, kp[None], vp[None], ikp[None], sp[None], ks[None], vs[None], iks[None], ss[None])
```

```python
import functools

import numpy as np
import jax
import jax.numpy as jnp
from jax import lax
from jax.experimental import pallas as pl
from jax.experimental.pallas import tpu as pltpu

F32 = jnp.float32
BF16 = jnp.bfloat16
I32 = jnp.int32

D_MODEL = 1024
PAST_LEN = 16384
PAGE_SIZE = 128
N_HEADS_A = 8
HEAD_DIM_A = 64
ROT_DIM = HEAD_DIM_A // 4
ROPE_THETA = 500000.0
N_HEADS_IDX = 4
IDX_DIM = 64
IDX_W_SCALE = (N_HEADS_IDX * IDX_DIM) ** -0.5
TOPK_MAX = 256
N_HEADS_B = 4
KEY_DIM_B = 128
VAL_DIM_B = 128
CHUNK_B = 16
WIDTH_A = N_HEADS_A * HEAD_DIM_A
WIDTH_B = N_HEADS_B * VAL_DIM_B
KEYW_B = N_HEADS_B * KEY_DIM_B
EPS = 1e-6
IN_SIZES = (WIDTH_A, WIDTH_A, WIDTH_A, N_HEADS_IDX * IDX_DIM, IDX_DIM, N_HEADS_IDX,
            KEYW_B, KEYW_B, WIDTH_B, WIDTH_B, D_MODEL, D_MODEL)
IN_OFFSETS = tuple(int(o) for o in np.cumsum(IN_SIZES)[:-1])

LANES = 128
SUBLANES = 8
VMEM_LIMIT_BYTES = 56 * 1024 * 1024
INT_MIN = np.int32(-2 ** 31)
NEG_BIG = -1e30
FF_CHUNK = 256
IN_CHUNK = 512
NT_DIMS = (((1,), (1,)), ((), ()))
LOG2E = 1.4426950408889634
ATTN_TILE = 256


def _params(*sem):
    return pltpu.CompilerParams(dimension_semantics=sem, vmem_limit_bytes=VMEM_LIMIT_BYTES)


def _float_key(x):
    u = pltpu.bitcast(x + 0.0, I32)
    return u ^ ((u >> 31) & np.int32(0x7FFFFFFF))


def _ffn_kernel(x_ref, g_ref, wg_ref, wu_ref, wd_ref, o_ref, h_scr, acc_scr):
    x = x_ref[...]
    ms = jnp.mean(x * x, axis=-1, keepdims=True)
    h_scr[...] = (x * lax.rsqrt(ms + EPS) * g_ref[...]).astype(BF16)
    acc_scr[...] = jnp.zeros_like(acc_scr)

    def chunk(c, carry):
        h = h_scr[...]
        cs = pl.ds(pl.multiple_of(c * FF_CHUNK, FF_CHUNK), FF_CHUNK)
        g = jnp.dot(h, wg_ref[:, cs], preferred_element_type=F32)
        u = jnp.dot(h, wu_ref[:, cs], preferred_element_type=F32)
        a = (g * jax.nn.sigmoid(g) * u).astype(BF16)
        acc_scr[...] += jnp.dot(a, wd_ref[cs, :], preferred_element_type=F32)
        return carry

    lax.fori_loop(0, wg_ref.shape[1] // FF_CHUNK, chunk, 0)
    o_ref[...] = x_ref[...] + 0.5 * acc_scr[...]


def _resident(shape):
    return pl.BlockSpec(shape, lambda i: (0,) * len(shape), pipeline_mode=pl.Buffered(1))


def _ffn(x, gain, wg, wu, wd, tm):
    n, d = x.shape
    return pl.pallas_call(
        _ffn_kernel,
        grid=(n // tm,),
        in_specs=[pl.BlockSpec((tm, d), lambda i: (i, 0)), _resident((1, d)),
                  _resident(wg.shape), _resident(wu.shape), _resident(wd.shape)],
        out_specs=pl.BlockSpec((tm, d), lambda i: (i, 0)),
        out_shape=jax.ShapeDtypeStruct((n, d), F32),
        scratch_shapes=[pltpu.VMEM((tm, d), BF16), pltpu.VMEM((tm, d), F32)],
        compiler_params=_params("parallel"),
        name="ffn",
    )(x, gain.reshape(1, d), wg, wu, wd)


_J_Q, _J_K, _J_V, _J_IDX, _J_HQ, _J_HF, _J_HI, _J_HG, _J_GA0, _J_GA1, _J_GB0, _J_GB1 = range(12)


def _rot(xc, c, sa, sb):
    return (xc * c + pltpu.roll(xc, LANES - ROT_DIM // 2, 1) * sa
            + pltpu.roll(xc, ROT_DIM // 2, 1) * sb)


def _inproj_kernel(x_ref, g_ref, w_ref, bd_ref, qg_ref, kg_ref, lb_ref, cos_ref, sa_ref, sb_ref,
                   qm_ref, k_ref, kb_ref, v_ref, vt_ref, iqm_ref, ikw_ref, ik2_ref,
                   hq_ref, kf_ref, lf_ref, hi_ref, hg_ref, ga_ref, gb_ref, h_scr, *, kv_cols):
    tm = x_ref.shape[0]
    x = x_ref[...]
    ms = jnp.mean(x * x, axis=-1, keepdims=True)
    h_scr[...] = (x * lax.rsqrt(ms + EPS) * g_ref[...]).astype(BF16)

    def chunk(jj):
        def run(fn):
            fn(jnp.dot(h_scr[...], w_ref[jj], preferred_element_type=F32))
        return run

    lane = lax.broadcasted_iota(I32, (tm, LANES), 1)
    lo_half = lane < HEAD_DIM_A

    def head_norm(t, gain):
        ms = jnp.dot((t * t).astype(BF16), bd_ref[...], preferred_element_type=F32)
        return t * lax.rsqrt(ms + EPS) * gain

    @chunk(_J_Q)
    def _(y):
        yn = head_norm(y, qg_ref[...])
        c, sa, sb = cos_ref[...], sa_ref[...], sb_ref[...]
        for p in range(WIDTH_A // LANES):
            r = _rot(yn[:, p * LANES:(p + 1) * LANES], c, sa, sb) * (HEAD_DIM_A ** -0.5 * LOG2E)
            qm_ref[:, (2 * p) * LANES:(2 * p + 1) * LANES] = jnp.where(lo_half, r, 0.0).astype(BF16)
            qm_ref[:, (2 * p + 1) * LANES:(2 * p + 2) * LANES] = jnp.where(lo_half, 0.0, r).astype(BF16)

    @chunk(_J_K)
    def _(y):
        yn = head_norm(y, kg_ref[...])
        c, sa, sb = cos_ref[...], sa_ref[...], sb_ref[...]
        for p in range(WIDTH_A // LANES):
            sl = slice(p * LANES, (p + 1) * LANES)
            r = _rot(yn[:, sl], c, sa, sb)
            if kv_cols:
                k_ref[0, sl, :] = r.T
            else:
                k_ref[:, sl] = r
            kb_ref[:, sl] = r.astype(BF16)

    @chunk(_J_V)
    def _(y):
        yt = y.T
        if kv_cols:
            v_ref[0] = yt
        else:
            v_ref[...] = y
        for u in range(vt_ref.shape[0]):
            vt_ref[u] = yt[:, u * vt_ref.shape[2]:(u + 1) * vt_ref.shape[2]].astype(BF16)

    @chunk(_J_IDX)
    def _(y):
        c, sa, sb = cos_ref[...], sa_ref[...], sb_ref[...]
        for p in range(2):
            r = _rot(y[:, p * LANES:(p + 1) * LANES], c, sa, sb)
            iqm_ref[:, (2 * p) * LANES:(2 * p + 1) * LANES] = jnp.where(lo_half, r, 0.0).astype(BF16)
            iqm_ref[:, (2 * p + 1) * LANES:(2 * p + 2) * LANES] = jnp.where(lo_half, 0.0, r).astype(BF16)
        r = _rot(y[:, 2 * LANES:3 * LANES], jnp.where(lo_half, c, 1.0),
                 jnp.where(lo_half, sa, 0.0), jnp.where(lo_half, sb, 0.0))
        if kv_cols:
            ikw_ref[0] = r.T
        else:
            ikw_ref[...] = r
        ik2_ref[...] = jnp.where(lo_half, r, pltpu.roll(r, HEAD_DIM_A, 1)).astype(BF16)

    @chunk(_J_HQ)
    def _(y):
        hq_ref[...] = (y * jax.nn.sigmoid(y)).astype(BF16)

    @chunk(_J_HF)
    def _(y):
        lb = lb_ref[...]
        lf_ref[...] = jnp.log(lb + (1.0 - lb) * jax.nn.sigmoid(y))
        kf_ref[...] = ((1.0 - lb) * jax.nn.sigmoid(-y)).astype(BF16)

    @chunk(_J_HI)
    def _(y):
        hi_ref[...] = y.astype(BF16)

    @chunk(_J_HG)
    def _(y):
        hg_ref[...] = (y * jax.nn.sigmoid(y)).astype(BF16)

    for jj, ref, half in ((_J_GA0, ga_ref, 0), (_J_GA1, ga_ref, 1), (_J_GB0, gb_ref, 0), (_J_GB1, gb_ref, 1)):
        @chunk(jj)
        def _(y, ref=ref, half=half):
            ref[:, half * IN_CHUNK:(half + 1) * IN_CHUNK] = jax.nn.sigmoid(y).astype(BF16)


def _inproj(x, gain, w12, bd, qg, kg, lb, cos_t, sa_t, sb_t, tm, kv_cols):
    n, d = x.shape
    nt = cos_t.shape[0] // tm
    row = lambda w: pl.BlockSpec((tm, w), lambda i: (i, 0))
    tab = pl.BlockSpec((tm, LANES), lambda i: (i % nt, 0))
    vt_tile = min(ATTN_TILE, tm)

    def rows(w, dt):
        return row(w), jax.ShapeDtypeStruct((n, w), dt)

    def kv(w):
        if kv_cols:
            return (pl.BlockSpec((1, w, tm), lambda i: (i // nt, 0, i % nt)),
                    jax.ShapeDtypeStruct((n // (nt * tm), w, nt * tm), F32))
        return rows(w, F32)

    outs = [rows(2 * WIDTH_A, BF16), kv(WIDTH_A), rows(WIDTH_A, BF16), kv(WIDTH_A),
            (pl.BlockSpec((tm // vt_tile, WIDTH_A, vt_tile), lambda i: (i, 0, 0)),
             jax.ShapeDtypeStruct((n // vt_tile, WIDTH_A, vt_tile), BF16)),
            rows(2 * N_HEADS_IDX * IDX_DIM, BF16), kv(LANES), rows(LANES, BF16),
            rows(KEYW_B, BF16), rows(KEYW_B, BF16), rows(KEYW_B, F32), rows(WIDTH_B, BF16), rows(WIDTH_B, BF16),
            rows(D_MODEL, BF16), rows(D_MODEL, BF16)]
    return pl.pallas_call(
        functools.partial(_inproj_kernel, kv_cols=kv_cols),
        grid=(n // tm,),
        in_specs=[row(d), _resident((1, d)), _resident(w12.shape), _resident(bd.shape),
                  _resident(qg.shape), _resident(kg.shape), _resident(lb.shape), tab, tab, tab],
        out_specs=[o[0] for o in outs],
        out_shape=[o[1] for o in outs],
        scratch_shapes=[pltpu.VMEM((tm, d), BF16)],
        compiler_params=_params("parallel"),
        name="inproj",
    )(x, gain.reshape(1, d), w12, bd, qg, kg, lb, cos_t, sa_t, sb_t)


def _tree(parts, op):
    while len(parts) > 1:
        parts = [op(a, b) for a, b in zip(parts[0::2], parts[1::2])] + ([parts[-1]] if len(parts) % 2 else [])
    return parts[0]


def _tree_sum(parts):
    return _tree(parts, jnp.add)


def _refine_threshold(cand, n_sel, count_fn, next_above, next_below):
    neg_flt_max_key = np.int32(-2139095040)
    key = jnp.maximum(cand, neg_flt_max_key)
    t = pltpu.bitcast(key ^ ((key >> 31) & np.int32(0x7FFFFFFF)), F32)

    def state(t):
        cge, cgt = count_fn(t, False), count_fn(t, True)
        lower = jnp.where(cand == INT_MIN, 0.0, -1.0)
        return cge, cgt, jnp.where(cgt >= n_sel, 1.0, jnp.where(cge < n_sel, lower, 0.0))

    def body(st):
        t, _, _, move = st
        t = jnp.where(move > 0.0, next_above(t), jnp.where(move < 0.0, next_below(t), t))
        return (t,) + state(t)

    t, cge, cgt, _ = lax.while_loop(lambda st: jnp.max(jnp.abs(st[3])) > 0.0, body, (t,) + state(t))
    return t, cge, cgt


def _search_threshold(count_ge, shape, n_sel):
    c0 = count_ge(jnp.zeros(shape, I32))
    ok = c0 >= n_sel
    cand = jnp.where(ok, np.int32(0), INT_MIN)
    ccnt = jnp.where(ok, c0, 0.0)

    def bit_body(i, carry):
        cand, ccnt = carry
        t = cand + (jnp.int32(1) << (30 - i))
        c = count_ge(t)
        ok = c >= n_sel
        return jnp.where(ok, t, cand), jnp.where(ok, c, ccnt)

    return lax.fori_loop(0, 31, bit_body, (cand, ccnt))


def _pattn_kernel(iqm_ref, ikw_ref, ik2_ref, qm_ref, k_ref, vt_ref, o_ref,
                  keys_scr, sc_scr, m_scr, a_scr, acc_scr, s_scr, p_scr, *, n_sel, tq):
    qi = pl.program_id(1)
    nkb = qi + 1
    krow = lax.broadcasted_iota(I32, (tq, tq), 0)
    qcol = lax.broadcasted_iota(I32, (tq, tq), 1)
    half = HEAD_DIM_A

    ikw_t = ikw_ref[0]
    wrow = [ikw_t[IDX_DIM + h:IDX_DIM + h + 1, :] * IDX_W_SCALE for h in range(N_HEADS_IDX)]

    def score_blk(kb, c):
        ik2 = ik2_ref[0, pl.ds(pl.multiple_of(kb * tq, tq), tq), :]
        sc = jnp.zeros((tq, tq), F32)
        for h in range(N_HEADS_IDX):
            d = lax.dot_general(ik2, iqm_ref[0, :, h * LANES:(h + 1) * LANES], NT_DIMS,
                                preferred_element_type=F32)
            sc = sc + jnp.maximum(d, 0.0) * wrow[h]
        off = jnp.where(kb < qi, jnp.int32(tq), jnp.int32(0))
        adm = krow <= qcol + off
        keys_scr[kb] = jnp.where(adm, _float_key(sc), INT_MIN)
        sc_scr[kb] = jnp.where(adm, sc, -jnp.inf)
        return c

    lax.fori_loop(0, nkb, score_blk, 0)
    keys_scr[nkb] = jnp.full((tq, tq), INT_MIN, I32)
    sc_scr[nkb] = jnp.full((tq, tq), -jnp.inf, F32)

    def over_rows(tile_fn, op, init):
        def block(kb):
            return _tree([tile_fn(kb, slice(r * SUBLANES, (r + 1) * SUBLANES))
                          for r in range(tq // SUBLANES)], op)

        def body(pi, acc):
            return op(acc, op(block(2 * pi), block(2 * pi + 1)))
        return lax.fori_loop(0, (nkb + 1) // 2, body, jnp.full((SUBLANES, tq), init, F32))

    def count_ge(t):
        acc = over_rows(lambda kb, rs: jnp.where(keys_scr[kb, rs, :] >= t, 1.0, 0.0), jnp.add, 0.0)
        return jnp.sum(acc, axis=0, keepdims=True)

    cand, _ = _search_threshold(count_ge, (1, tq), n_sel)

    def fcount(t, strict):
        hit = (lambda s: s > t) if strict else (lambda s: s >= t)
        acc = over_rows(lambda kb, rs: jnp.where(hit(sc_scr[kb, rs, :]), 1.0, 0.0), jnp.add, 0.0)
        return jnp.sum(acc, axis=0, keepdims=True)

    def next_above(t):
        def tile(kb, rs):
            s = sc_scr[kb, rs, :]
            return jnp.where(s > t, s, jnp.inf)
        return jnp.min(over_rows(tile, jnp.minimum, jnp.inf), axis=0, keepdims=True)

    def next_below(t):
        def tile(kb, rs):
            s = sc_scr[kb, rs, :]
            return jnp.where(s < t, s, -jnp.inf)
        return jnp.max(over_rows(tile, jnp.maximum, -jnp.inf), axis=0, keepdims=True)

    thr, cge, cgt = _refine_threshold(cand, n_sel, fcount, next_above, next_below)

    @pl.when(jnp.max(cge) > n_sel)
    def _():
        need = n_sel - cgt
        lower = jnp.where(qcol < krow, 1.0, 0.0).astype(BF16)

        def tie_blk(kb, seen):
            s = sc_scr[kb]
            tf = jnp.where(s == thr, 1.0, 0.0)
            rank = jnp.dot(lower, tf.astype(BF16), preferred_element_type=F32) + seen
            drop = jnp.where(rank >= need, tf, 0.0)
            sc_scr[kb] = jnp.where(drop > 0.0, -jnp.inf, s)
            return seen + jnp.sum(tf, axis=0, keepdims=True)

        lax.fori_loop(0, nkb, tie_blk, jnp.zeros((1, tq), F32))

    m_scr[...] = jnp.full(m_scr.shape, NEG_BIG, F32)
    acc_scr[...] = jnp.zeros(acc_scr.shape, F32)
    ones_half = jnp.ones((half, tq), BF16)

    def attn_blk(kb, c):
        bias = jnp.where(sc_scr[kb] >= thr, 0.0, NEG_BIG)
        ks = pl.ds(pl.multiple_of(kb * tq, tq), tq)
        for h in range(N_HEADS_A):
            pr = h // 2
            s_scr[h] = lax.dot_general(k_ref[0, ks, pr * LANES:(pr + 1) * LANES],
                                       qm_ref[0, :, h * LANES:(h + 1) * LANES], NT_DIMS,
                                       preferred_element_type=F32) + bias
        for h in range(N_HEADS_A):
            s = s_scr[h]
            m_old = m_scr[h]
            m_new = jnp.maximum(m_old, jnp.max(s, axis=0, keepdims=True))
            a_scr[h] = jnp.exp2(m_old - m_new)
            p_scr[h] = jnp.exp2(s - m_new).astype(BF16)
            m_scr[h] = m_new
        for h in range(N_HEADS_A):
            pr = h // 2
            vt = vt_ref[0, kb, pr * LANES:(pr + 1) * LANES, :]
            lhs = (jnp.concatenate([vt[:half], ones_half], axis=0) if h % 2 == 0
                   else jnp.concatenate([ones_half, vt[half:]], axis=0))
            acc_scr[h] = a_scr[h] * acc_scr[h] + jnp.dot(lhs, p_scr[h], preferred_element_type=F32)
        return c

    lax.fori_loop(0, nkb, attn_blk, 0)
    for pr in range(N_HEADS_A // 2):
        ae, ao = acc_scr[2 * pr], acc_scr[2 * pr + 1]
        ot = jnp.concatenate([ae[:half] / ae[half:], ao[half:] / ao[:half]], axis=0)
        o_ref[0, :, pr * LANES:(pr + 1) * LANES] = ot.T


def _prompt_attention(iqm, ikw, ik2, qm, kb, vt, n_sel, tq):
    b, t, _ = qm.shape
    nq = t // tq
    qblk = lambda w: pl.BlockSpec((1, tq, w), lambda bi, qi: (bi, qi, 0))
    full = lambda w: pl.BlockSpec((1, t, w), lambda bi, qi: (bi, 0, 0))
    return pl.pallas_call(
        functools.partial(_pattn_kernel, n_sel=n_sel, tq=tq),
        grid=(b, nq),
        in_specs=[qblk(iqm.shape[2]), pl.BlockSpec((1, LANES, tq), lambda bi, qi: (bi, 0, qi)), full(LANES),
                  qblk(qm.shape[2]), full(WIDTH_A),
                  pl.BlockSpec((1, nq, WIDTH_A, tq), lambda bi, qi: (bi, 0, 0, 0))],
        out_specs=qblk(WIDTH_A),
        out_shape=jax.ShapeDtypeStruct((b, t, WIDTH_A), F32),
        scratch_shapes=[
            pltpu.VMEM((nq + 1, tq, tq), I32),
            pltpu.VMEM((nq + 1, tq, tq), F32),
            pltpu.VMEM((N_HEADS_A, 1, tq), F32),
            pltpu.VMEM((N_HEADS_A, 1, tq), F32),
            pltpu.VMEM((N_HEADS_A, LANES, tq), F32),
            pltpu.VMEM((N_HEADS_A, tq, tq), F32),
            pltpu.VMEM((N_HEADS_A, tq, tq), BF16),
        ],
        compiler_params=_params("parallel", "arbitrary"),
        name="pattn",
    )(iqm, ikw, ik2, qm, kb, vt)


def _hgrn_kernel(q_ref, k_ref, g_ref, v_ref, s0_ref, o_ref, sn_ref, s_scr, *, tc):
    ti = pl.program_id(1)
    c_ = CHUNK_B

    streams = [(bi, h) for bi in range(q_ref.shape[0]) for h in range(N_HEADS_B)]

    @pl.when(ti == 0)
    def _():
        for si, (bi, h) in enumerate(streams):
            s_scr[si] = s0_ref[bi, h].T

    rowi = lax.broadcasted_iota(I32, (c_, LANES), 0)
    lanei = lax.broadcasted_iota(I32, (c_, LANES), 1)

    def chunk(ci, carry):
        rs = pl.ds(pl.multiple_of(ci * c_, c_), c_)
        for si, (bi, h) in enumerate(streams):
            hs = slice(h * LANES, (h + 1) * LANES)
            q = q_ref[bi, rs, hs].astype(F32)
            k = k_ref[bi, rs, hs].astype(F32)
            v = v_ref[bi, rs, hs]
            b = g_ref[bi, rs, hs] * LOG2E
            for sh in (1, 2, 4, 8):
                b = b + jnp.where(rowi >= sh, pltpu.roll(b, sh, 0), 0.0)
            b_last = b[c_ - 1:c_, :]
            st_old = s_scr[si]
            o = lax.dot_general((q * jnp.exp2(b)).astype(BF16), st_old.astype(BF16), NT_DIMS,
                                preferred_element_type=F32)
            cols = []
            for t in range(c_):
                e = jnp.exp2(jnp.minimum(b[t:t + 1, :] - b, 0.0))
                col = jnp.sum(e * k * q[t:t + 1, :], axis=1, keepdims=True)
                cols.append(jnp.where(lanei == t, col, 0.0))
            att = jnp.where(rowi <= lanei, _tree_sum(cols), 0.0).T[:c_]
            o_ref[bi, rs, hs] = o + jnp.dot(att.astype(BF16), v, preferred_element_type=F32)
            kd = k * jnp.exp2(b_last - b)
            upd = jnp.dot(v.astype(F32).T.astype(BF16), kd.astype(BF16), preferred_element_type=F32)
            s_scr[si] = st_old * jnp.exp2(b_last) + upd
        return carry

    lax.fori_loop(0, tc // c_, chunk, 0)

    @pl.when(ti == pl.num_programs(1) - 1)
    def _():
        for si, (bi, h) in enumerate(streams):
            sn_ref[bi, h] = s_scr[si].T


HGRN_BATCH_GROUP = 4


def _hgrn(q, k, g, v, s0, tc):
    b, t, w = q.shape
    bg = min(HGRN_BATCH_GROUP, b)
    blk = pl.BlockSpec((bg, tc, w), lambda bi, ti: (bi, ti, 0))
    st = pl.BlockSpec((bg, N_HEADS_B, KEY_DIM_B, VAL_DIM_B), lambda bi, ti: (bi, 0, 0, 0))
    return pl.pallas_call(
        functools.partial(_hgrn_kernel, tc=tc),
        grid=(b // bg, t // tc),
        in_specs=[blk, blk, blk, blk, st],
        out_specs=[blk, st],
        out_shape=[jax.ShapeDtypeStruct((b, t, w), F32), jax.ShapeDtypeStruct(s0.shape, F32)],
        scratch_shapes=[pltpu.VMEM((bg * N_HEADS_B, KEY_DIM_B, VAL_DIM_B), F32)],
        compiler_params=_params("parallel", "arbitrary"),
        name="hgrn",
    )(q, k, g, v, s0)


def _merge_kernel(attn_ref, o_ref, hg_ref, ga_ref, gb_ref, x_ref, on_ref, wpa_ref, wph_ref, wo_ref, y_ref):
    o = o_ref[...]
    parts = []
    for h in range(N_HEADS_B):
        oh = o[:, h * LANES:(h + 1) * LANES]
        ms = jnp.mean(oh * oh, axis=-1, keepdims=True)
        parts.append(oh * lax.rsqrt(ms + EPS))
    on = jnp.concatenate(parts, axis=1) * on_ref[...] * hg_ref[...]
    pa = jnp.dot(attn_ref[...].astype(BF16), wpa_ref[...], preferred_element_type=F32)
    ph = jnp.dot(on.astype(BF16), wph_ref[...], preferred_element_type=F32)
    merged = ga_ref[...] * pa + gb_ref[...] * ph
    y_ref[...] = x_ref[...] + jnp.dot(merged.astype(BF16), wo_ref[...], preferred_element_type=F32)


def _merge(attn, o, hg, ga, gb, x, onorm, wpa, wph, wo, tm):
    n, d = x.shape
    row = lambda w: pl.BlockSpec((tm, w), lambda i: (i, 0))
    const = lambda r, w: pl.BlockSpec((r, w), lambda i: (0, 0))
    return pl.pallas_call(
        _merge_kernel,
        grid=(n // tm,),
        in_specs=[row(WIDTH_A), row(WIDTH_B), row(WIDTH_B), row(d), row(d), row(d),
                  const(1, WIDTH_B), const(WIDTH_A, d), const(WIDTH_B, d), const(d, d)],
        out_specs=row(d),
        out_shape=jax.ShapeDtypeStruct((n, d), F32),
        compiler_params=_params("parallel"),
        name="merge",
    )(attn, o, hg, ga, gb, x, onorm, wpa, wph, wo)


def _rot_tables(pos):
    r = pos.shape[0]
    inv = ROPE_THETA ** (-jnp.arange(0, ROT_DIM, 2, dtype=F32) / ROT_DIM)
    ang = pos.astype(F32)[:, None] * inv[None, :]
    cos, sin = jnp.cos(ang), jnp.sin(ang)
    half = ROT_DIM // 2
    rest = HEAD_DIM_A - ROT_DIM
    one, z8, zr = jnp.ones((r, rest), F32), jnp.zeros((r, half), F32), jnp.zeros((r, rest), F32)
    c64 = jnp.concatenate([cos, cos, one], axis=1)
    a64 = jnp.concatenate([-sin, z8, zr], axis=1)
    b64 = jnp.concatenate([z8, sin, zr], axis=1)
    return tuple(jnp.concatenate([t, t], axis=1) for t in (c64, a64, b64))


def _prep_weights(ffn1_norm, ffn1_w_gate, ffn1_w_up, ffn1_w_down, mix_norm, w_in, q_norm, k_norm,
                  hgrn_lb, hgrn_o_norm, w_proj_attn, w_proj_hgrn, w_out,
                  ffn2_norm, ffn2_w_gate, ffn2_w_up, ffn2_w_down):
    l = 0
    q, k, v, iq, ik, iw, hq, hf, hi, hg, ga, gb = jnp.split(w_in[l], IN_OFFSETS, axis=1)
    pad = jnp.zeros((D_MODEL, IN_CHUNK - iq.shape[1] - ik.shape[1] - iw.shape[1]), F32)
    idx = jnp.concatenate([iq, ik, iw, pad], axis=1)
    w12 = jnp.stack([q, k, v, idx, hq, hf, hi, hg,
                     ga[:, :IN_CHUNK], ga[:, IN_CHUNK:], gb[:, :IN_CHUNK], gb[:, IN_CHUNK:]]).astype(BF16)
    head = np.arange(WIDTH_A) // HEAD_DIM_A
    bd = jnp.asarray((head[:, None] == head[None, :]).astype(np.float32) / HEAD_DIM_A, BF16)
    lb = jnp.cumsum(jax.nn.softmax(hgrn_lb.astype(F32), axis=0), axis=0)[l].reshape(1, KEYW_B)
    cols = rows = lambda a: a.astype(BF16)
    return dict(
        ffn1=(ffn1_norm[l], cols(ffn1_w_gate[l]), cols(ffn1_w_up[l]), rows(ffn1_w_down[l])),
        ffn2=(ffn2_norm[l], cols(ffn2_w_gate[l]), cols(ffn2_w_up[l]), rows(ffn2_w_down[l])),
        mix_norm=mix_norm[l], w12=w12, bd=bd,
        qg=jnp.tile(q_norm[l], N_HEADS_A).reshape(1, WIDTH_A),
        kg=jnp.tile(k_norm[l], N_HEADS_A).reshape(1, WIDTH_A),
        lb=lb,
        onorm=jnp.tile(hgrn_o_norm[l], N_HEADS_B).reshape(1, WIDTH_B),
        wpa=w_proj_attn[l].astype(BF16), wph=w_proj_hgrn[l].astype(BF16), wo=w_out[l].astype(BF16),
    )


def _pre(x, w, tabs, tm_ffn, tm_in, kv_cols):
    x1 = _ffn(x, *w["ffn1"], tm_ffn)
    names = ("qm", "k", "kb", "v", "vt", "iqm", "ikw", "ik2", "hq", "kf", "lf", "hi", "hg", "ga", "gb")
    outs = _inproj(x1, w["mix_norm"], w["w12"], w["bd"], w["qg"], w["kg"], w["lb"], *tabs, tm_in, kv_cols)
    return x1, dict(zip(names, outs))


def _post(x1, attn, o, p, w, tm_merge, tm_ffn):
    x2 = _merge(attn, o, p["hg"], p["ga"], p["gb"], x1, w["onorm"], w["wpa"], w["wph"], w["wo"], tm_merge)
    return _ffn(x2, *w["ffn2"], tm_ffn)


def _prompt_layer(x_prompt, w):
    b, t, d = x_prompt.shape
    n = b * t
    tabs = _rot_tables(jnp.arange(t, dtype=I32))
    x1, p = _pre(x_prompt.reshape(n, d), w, tabs, min(1024, n), min(512, t), True)
    r3 = lambda a: a.reshape(b, t, a.shape[-1])
    n_sel = min(TOPK_MAX, t // 4)
    tq = min(ATTN_TILE, t)
    vt = p["vt"].reshape(b, t // tq, WIDTH_A, tq)
    attn = _prompt_attention(r3(p["iqm"]), p["ikw"], r3(p["ik2"]), r3(p["qm"]), r3(p["kb"]), vt, n_sel, tq)
    s0 = jnp.zeros((b, N_HEADS_B, KEY_DIM_B, VAL_DIM_B), F32)
    o, s_new = _hgrn(r3(p["hq"]), r3(p["kf"]), r3(p["lf"]), r3(p["hi"]), s0, min(256, t))
    y = _post(x1, attn.reshape(n, WIDTH_A), o.reshape(n, WIDTH_B), p, w, min(512, n), min(1024, n))
    heads = lambda a: a.reshape(b, N_HEADS_A, HEAD_DIM_A, t).transpose(0, 3, 1, 2)
    return (y.reshape(b, t, d), heads(p["k"]), heads(p["v"]),
            p["ikw"][:, :IDX_DIM, :].transpose(0, 2, 1), s_new)


KV_PAGES_PER_STEP = 16
IDX_PAGE_UNROLL = 32


def _sidx_kernel(pt_ref, iq2_ref, w2_ref, iknew_ref, idx_hbm, sc_ref, thr_ref, keys_scr, page_buf, sems,
                 *, n_pages, n_sel, unroll):
    bi = pl.program_id(0)
    slot = bi % 2
    nblk = sc_ref.shape[1]

    def page_copy(b_, p, sl):
        return pltpu.make_async_copy(idx_hbm.at[pt_ref[b_, p]], page_buf.at[sl, p], sems.at[sl])

    def start_pages(b_, sl):
        def body(p, c):
            page_copy(b_, p, sl).start()
            return c
        lax.fori_loop(0, n_pages, body, 0)

    @pl.when(bi == 0)
    def _():
        start_pages(0, 0)

    @pl.when(bi + 1 < pl.num_programs(0))
    def _():
        start_pages(bi + 1, 1 - slot)

    pltpu.make_async_copy(idx_hbm.at[pl.ds(0, n_pages)], page_buf.at[slot], sems.at[slot]).wait()
    iq2 = iq2_ref[0]
    w2 = w2_ref[0] * IDX_W_SCALE
    tq = iq2.shape[0] // N_HEADS_IDX

    def block_scores(ik_t):
        d = jnp.dot(iq2, ik_t, preferred_element_type=F32)
        r = jnp.maximum(d, 0.0) * w2
        sc = r[0:tq]
        for h in range(1, N_HEADS_IDX):
            sc = sc + r[h * tq:(h + 1) * tq]
        return sc

    def score_pages(g, c):
        for j in range(unroll):
            p = g * unroll + j
            sc = block_scores(page_buf[slot, p].astype(BF16))
            sc_ref[0, p] = sc
            keys_scr[p] = _float_key(sc)
        return c

    lax.fori_loop(0, n_pages // unroll, score_pages, 0)

    def finish():
        lane = lax.broadcasted_iota(I32, (tq, LANES), 1)
        row = lax.broadcasted_iota(I32, (tq, LANES), 0)
        sc = block_scores(iknew_ref[0])
        sc_ref[0, n_pages] = jnp.where(lane <= row, sc, -jnp.inf)
        keys_scr[n_pages] = jnp.where(lane <= row, _float_key(sc), INT_MIN)
        for u in range(n_pages + 1, nblk):
            sc_ref[0, u] = jnp.full((tq, LANES), -jnp.inf, F32)
            keys_scr[u] = jnp.full((tq, LANES), INT_MIN, I32)

        def over_blocks(tile_fn, op):
            accs = [tile_fn(u) for u in range(SUBLANES)]
            for u in range(SUBLANES, nblk):
                accs[u % SUBLANES] = op(accs[u % SUBLANES], tile_fn(u))
            return _tree(accs, op)

        def count_ge(t):
            return jnp.sum(over_blocks(lambda u: jnp.where(keys_scr[u] >= t, 1.0, 0.0), jnp.add),
                           axis=1, keepdims=True)

        def fcount(t, strict):
            hit = (lambda x: x > t) if strict else (lambda x: x >= t)
            return jnp.sum(over_blocks(lambda u: jnp.where(hit(sc_ref[0, u]), 1.0, 0.0), jnp.add),
                           axis=1, keepdims=True)

        def next_above(t):
            def tile(u):
                x = sc_ref[0, u]
                return jnp.where(x > t, x, jnp.inf)
            return jnp.min(over_blocks(tile, jnp.minimum), axis=1, keepdims=True)

        def next_below(t):
            def tile(u):
                x = sc_ref[0, u]
                return jnp.where(x < t, x, -jnp.inf)
            return jnp.max(over_blocks(tile, jnp.maximum), axis=1, keepdims=True)

        cand, _ = _search_threshold(count_ge, (tq, 1), n_sel)
        thr, cge, cgt = _refine_threshold(cand, n_sel, fcount, next_above, next_below)
        thr_ref[0] = jnp.broadcast_to(thr, (tq, LANES))

        @pl.when(jnp.max(cge) > n_sel)
        def _():
            need = n_sel - cgt
            r2 = lax.broadcasted_iota(I32, (LANES, LANES), 0)
            c2 = lax.broadcasted_iota(I32, (LANES, LANES), 1)
            upper = jnp.where(r2 < c2, 1.0, 0.0)

            def tie_blk(u, seen):
                x = sc_ref[0, u]
                tf = jnp.where(x == thr, 1.0, 0.0)
                rank = jnp.dot(tf, upper, preferred_element_type=F32) + seen
                drop = jnp.where(rank >= need, tf, 0.0)
                sc_ref[0, u] = jnp.where(drop > 0.0, -jnp.inf, x)
                return seen + jnp.sum(tf, axis=1, keepdims=True)

            lax.fori_loop(0, n_pages + 1, tie_blk, jnp.zeros((tq, 1), F32))

    finish()


def _sample_index(page_table, iq2, w2, iknew_t, idx_pages_t, n_sel):
    b, n_pages = page_table.shape
    tq = iq2.shape[1] // N_HEADS_IDX
    nblk = -(-(n_pages + 1) // SUBLANES) * SUBLANES
    per_b = lambda r, w: pl.BlockSpec((1, r, w), lambda bi, pt: (bi, 0, 0))
    return pl.pallas_call(
        functools.partial(_sidx_kernel, n_pages=n_pages, n_sel=n_sel, unroll=min(IDX_PAGE_UNROLL, n_pages)),
        grid_spec=pltpu.PrefetchScalarGridSpec(
            num_scalar_prefetch=1,
            grid=(b,),
            in_specs=[per_b(N_HEADS_IDX * tq, IDX_DIM), per_b(N_HEADS_IDX * tq, LANES), per_b(IDX_DIM, PAGE_SIZE),
                      pl.BlockSpec(memory_space=pl.ANY)],
            out_specs=[pl.BlockSpec((1, nblk, tq, LANES), lambda bi, pt: (bi, 0, 0, 0)),
                       pl.BlockSpec((1, tq, LANES), lambda bi, pt: (bi, 0, 0))],
            scratch_shapes=[pltpu.VMEM((nblk, tq, LANES), I32),
                            pltpu.VMEM((2, n_pages, IDX_DIM, PAGE_SIZE), F32),
                            pltpu.SemaphoreType.DMA((2,))],
        ),
        out_shape=[jax.ShapeDtypeStruct((b, nblk, tq, LANES), F32), jax.ShapeDtypeStruct((b, tq, LANES), F32)],
        compiler_params=_params("arbitrary"),
        name="sidx",
    )(page_table, iq2, w2, iknew_t, idx_pages_t)


def _sattn_kernel(pt_ref, qbd_ref, keys_ref, thr_ref, knew_ref, vnew_ref, *rest, ppb, n_pages):
    del pt_ref
    kpages, vpages = rest[:ppb], rest[ppb:2 * ppb]
    o_ref, m_scr, l_scr, acc_scr = rest[2 * ppb:]
    s = pl.program_id(1)
    nh = N_HEADS_A
    qbd = qbd_ref[0]
    tq = qbd.shape[0] // nh
    thr = jnp.concatenate([thr_ref[0, :, :1]] * nh, axis=0)

    @pl.when(s == 0)
    def _():
        m_scr[...] = jnp.full(m_scr.shape, NEG_BIG, F32)
        l_scr[...] = jnp.zeros(l_scr.shape, F32)
        acc_scr[...] = jnp.zeros(acc_scr.shape, F32)

    def pages_update(kts, vts, blks):
        kt = jnp.concatenate([a.astype(BF16) for a in kts], axis=1)
        vt = jnp.concatenate([a.astype(BF16) for a in vts], axis=1)
        kk = jnp.concatenate([keys_ref[0, blk] for blk in blks], axis=1)
        sc = jnp.dot(qbd, kt, preferred_element_type=F32)
        sc = jnp.where(jnp.concatenate([kk] * nh, axis=0) >= thr, sc, NEG_BIG)
        m_old = m_scr[:, :1]
        m_new = jnp.maximum(m_old, jnp.max(sc, axis=1, keepdims=True))
        alpha = jnp.exp2(m_old - m_new)
        p = jnp.exp2(sc - m_new)
        l_scr[...] = alpha * l_scr[...] + jnp.sum(p, axis=1, keepdims=True)
        pv = lax.dot_general(p.astype(BF16), vt, NT_DIMS, preferred_element_type=F32)
        acc_scr[...] = alpha * acc_scr[...] + pv
        m_scr[...] = jnp.broadcast_to(m_new, m_scr.shape)

    pages_update([r[...] for r in kpages], [r[...] for r in vpages], [s * ppb + j for j in range(ppb)])

    @pl.when(s == pl.num_programs(1) - 1)
    def _():
        pages_update([knew_ref[0]], [vnew_ref[0]], [n_pages])
        acc = acc_scr[...]
        o = jnp.concatenate([acc[h * tq:(h + 1) * tq, h * HEAD_DIM_A:(h + 1) * HEAD_DIM_A] for h in range(nh)],
                            axis=0)
        o_ref[0] = o / l_scr[:, :HEAD_DIM_A]


def _sample_attention(page_table, qbd, keys, thr, knew_t, vnew_t, k_pages_t, v_pages_t):
    b, n_pages = page_table.shape
    ppb = min(KV_PAGES_PER_STEP, n_pages)
    rows = qbd.shape[1]
    per_b3 = lambda r, w: pl.BlockSpec((1, r, w), lambda bi, s, pt: (bi, 0, 0))
    page = lambda j: pl.BlockSpec((None, WIDTH_A, PAGE_SIZE), lambda bi, s, pt, j=j: (pt[bi, s * ppb + j], 0, 0))
    return pl.pallas_call(
        functools.partial(_sattn_kernel, ppb=ppb, n_pages=n_pages),
        grid_spec=pltpu.PrefetchScalarGridSpec(
            num_scalar_prefetch=1,
            grid=(b, n_pages // ppb),
            in_specs=[per_b3(rows, WIDTH_A),
                      pl.BlockSpec((1,) + keys.shape[1:], lambda bi, s, pt: (bi, 0, 0, 0)),
                      per_b3(thr.shape[1], LANES), per_b3(WIDTH_A, PAGE_SIZE), per_b3(WIDTH_A, PAGE_SIZE)]
            + [page(j) for j in range(ppb)] * 2,
            out_specs=per_b3(rows, HEAD_DIM_A),
            scratch_shapes=[pltpu.VMEM((rows, LANES), F32), pltpu.VMEM((rows, LANES), F32),
                            pltpu.VMEM((rows, WIDTH_A), F32)],
        ),
        out_shape=jax.ShapeDtypeStruct((b, rows, HEAD_DIM_A), F32),
        compiler_params=_params("parallel", "arbitrary"),
        name="sattn",
    )(page_table, qbd, keys, thr, knew_t, vnew_t, *([k_pages_t] * ppb), *([v_pages_t] * ppb))


def _unmask_heads(xm, n_heads):
    x = xm.reshape(xm.shape[0], n_heads, LANES)
    return jnp.stack([x[:, h, (h % 2) * HEAD_DIM_A:(h % 2 + 1) * HEAD_DIM_A] for h in range(n_heads)], axis=1)


def _sample_layer(x_sample, cache_k, cache_v, cache_idx_k, state, page_table, w):
    b, t, d = x_sample.shape
    n = b * t
    n_pages = page_table.shape[1]
    past = n_pages * PAGE_SIZE
    pos = past + jnp.arange(t, dtype=I32)
    x1, p = _pre(x_sample.reshape(n, d), w, _rot_tables(jnp.tile(pos, b)), n, n, False)

    iq2 = _unmask_heads(p["iqm"], N_HEADS_IDX).reshape(b, t, N_HEADS_IDX, IDX_DIM)
    iq2 = iq2.transpose(0, 2, 1, 3).reshape(b, N_HEADS_IDX * t, IDX_DIM)
    w2 = p["ikw"][:, IDX_DIM:IDX_DIM + N_HEADS_IDX].reshape(b, t, N_HEADS_IDX).transpose(0, 2, 1)
    w2 = jnp.broadcast_to(w2.reshape(b, N_HEADS_IDX * t, 1), (b, N_HEADS_IDX * t, LANES))
    new_page = lambda a: jnp.pad(a.reshape(b, t, -1).transpose(0, 2, 1), ((0, 0), (0, 0), (0, PAGE_SIZE - t)))
    n_sel = min(TOPK_MAX, (past + t) // 4)
    keys, thr = _sample_index(page_table, iq2, w2, new_page(p["ik2"][:, :IDX_DIM]),
                              cache_idx_k[0].transpose(0, 2, 1), n_sel)

    q2 = _unmask_heads(p["qm"], N_HEADS_A).reshape(b, t, N_HEADS_A, HEAD_DIM_A).transpose(0, 2, 1, 3)
    eye = jnp.eye(N_HEADS_A, dtype=q2.dtype)
    qbd = (q2[:, :, :, None, :] * eye[None, :, None, :, None]).reshape(b, N_HEADS_A * t, WIDTH_A)
    pages_t = lambda c: c[0].transpose(0, 2, 3, 1).reshape(-1, WIDTH_A, PAGE_SIZE)
    o2 = _sample_attention(page_table, qbd, keys, thr, new_page(p["k"]), new_page(p["v"]),
                           pages_t(cache_k), pages_t(cache_v))
    attn = o2.reshape(b, N_HEADS_A, t, HEAD_DIM_A).transpose(0, 2, 1, 3).reshape(n, WIDTH_A)

    tp = -(-t // CHUNK_B) * CHUNK_B
    r3 = lambda a: jnp.pad(a.reshape(b, t, a.shape[-1]), ((0, 0), (0, tp - t), (0, 0)))
    o, s_new = _hgrn(r3(p["hq"]), r3(p["kf"]), r3(p["lf"]), r3(p["hi"]), state, tp)
    y = _post(x1, attn, o[:, :t].reshape(n, WIDTH_B), p, w, n, n)
    return (y.reshape(b, t, d),
            p["k"].reshape(b, t, N_HEADS_A, HEAD_DIM_A), p["v"].reshape(b, t, N_HEADS_A, HEAD_DIM_A),
            p["ikw"][:, :IDX_DIM].reshape(b, t, IDX_DIM), s_new)


def kernel(x_prompt, x_sample, cache_k, cache_v, cache_idx_k, state_hgrn, page_table,
           ffn1_norm, ffn1_w_gate, ffn1_w_up, ffn1_w_down, mix_norm, w_in, q_norm, k_norm,
           hgrn_lb, hgrn_o_norm, w_proj_attn, w_proj_hgrn, w_out,
           ffn2_norm, ffn2_w_gate, ffn2_w_up, ffn2_w_down):
    w = _prep_weights(ffn1_norm, ffn1_w_gate, ffn1_w_up, ffn1_w_down, mix_norm, w_in, q_norm, k_norm,
                      hgrn_lb, hgrn_o_norm, w_proj_attn, w_proj_hgrn, w_out,
                      ffn2_norm, ffn2_w_gate, ffn2_w_up, ffn2_w_down)
    yp, kp, vp, ikp, sp = _prompt_layer(x_prompt, w)
    ys, ks, vs, iks, ss = _sample_layer(x_sample, cache_k, cache_v, cache_idx_k, state_hgrn[0], page_table, w)
    return (yp, ys, kp[None], vp[None], ikp[None], sp[None], ks[None], vs[None], iks[None], ss[None])
```

```python
import functools

import numpy as np
import jax
import jax.numpy as jnp
from jax import lax
from jax.experimental import pallas as pl
from jax.experimental.pallas import tpu as pltpu

F32 = jnp.float32
BF16 = jnp.bfloat16
I32 = jnp.int32

D_MODEL = 1024
PAST_LEN = 16384
PAGE_SIZE = 128
N_HEADS_A = 8
HEAD_DIM_A = 64
ROT_DIM = HEAD_DIM_A // 4
ROPE_THETA = 500000.0
N_HEADS_IDX = 4
IDX_DIM = 64
IDX_W_SCALE = (N_HEADS_IDX * IDX_DIM) ** -0.5
TOPK_MAX = 256
N_HEADS_B = 4
KEY_DIM_B = 128
VAL_DIM_B = 128
CHUNK_B = 16
WIDTH_A = N_HEADS_A * HEAD_DIM_A
WIDTH_B = N_HEADS_B * VAL_DIM_B
KEYW_B = N_HEADS_B * KEY_DIM_B
EPS = 1e-6
IN_SIZES = (WIDTH_A, WIDTH_A, WIDTH_A, N_HEADS_IDX * IDX_DIM, IDX_DIM, N_HEADS_IDX,
            KEYW_B, KEYW_B, WIDTH_B, WIDTH_B, D_MODEL, D_MODEL)
IN_OFFSETS = tuple(int(o) for o in np.cumsum(IN_SIZES)[:-1])

LANES = 128
SUBLANES = 8
VMEM_LIMIT_BYTES = 56 * 1024 * 1024
INT_MIN = np.int32(-2 ** 31)
NEG_BIG = -1e30
FF_CHUNK = 256
IN_CHUNK = 512
NT_DIMS = (((1,), (1,)), ((), ()))
LOG2E = 1.4426950408889634
ATTN_TILE = 256


def _params(*sem):
    return pltpu.CompilerParams(dimension_semantics=sem, vmem_limit_bytes=VMEM_LIMIT_BYTES)


def _float_key(x):
    u = pltpu.bitcast(x + 0.0, I32)
    return u ^ ((u >> 31) & np.int32(0x7FFFFFFF))


def _ffn_kernel(x_ref, g_ref, wg_ref, wu_ref, wd_ref, o_ref, h_scr, acc_scr):
    x = x_ref[...]
    ms = jnp.mean(x * x, axis=-1, keepdims=True)
    h_scr[...] = (x * lax.rsqrt(ms + EPS) * g_ref[...]).astype(BF16)
    acc_scr[...] = jnp.zeros_like(acc_scr)

    def chunk(c, carry):
        h = h_scr[...]
        cs = pl.ds(pl.multiple_of(c * FF_CHUNK, FF_CHUNK), FF_CHUNK)
        g = jnp.dot(h, wg_ref[:, cs], preferred_element_type=F32)
        u = jnp.dot(h, wu_ref[:, cs], preferred_element_type=F32)
        a = (g * jax.nn.sigmoid(g) * u).astype(BF16)
        acc_scr[...] += jnp.dot(a, wd_ref[cs, :], preferred_element_type=F32)
        return carry

    lax.fori_loop(0, wg_ref.shape[1] // FF_CHUNK, chunk, 0)
    o_ref[...] = x_ref[...] + 0.5 * acc_scr[...]


def _resident(shape):
    return pl.BlockSpec(shape, lambda i: (0,) * len(shape), pipeline_mode=pl.Buffered(1))


def _ffn(x, gain, wg, wu, wd, tm):
    n, d = x.shape
    return pl.pallas_call(
        _ffn_kernel,
        grid=(n // tm,),
        in_specs=[pl.BlockSpec((tm, d), lambda i: (i, 0)), _resident((1, d)),
                  _resident(wg.shape), _resident(wu.shape), _resident(wd.shape)],
        out_specs=pl.BlockSpec((tm, d), lambda i: (i, 0)),
        out_shape=jax.ShapeDtypeStruct((n, d), F32),
        scratch_shapes=[pltpu.VMEM((tm, d), BF16), pltpu.VMEM((tm, d), F32)],
        compiler_params=_params("parallel"),
        name="ffn",
    )(x, gain.reshape(1, d), wg, wu, wd)


_J_Q, _J_K, _J_V, _J_IDX, _J_HQ, _J_HF, _J_HI, _J_HG, _J_GA0, _J_GA1, _J_GB0, _J_GB1 = range(12)


def _rot(xc, c, sa, sb):
    return (xc * c + pltpu.roll(xc, LANES - ROT_DIM // 2, 1) * sa
            + pltpu.roll(xc, ROT_DIM // 2, 1) * sb)


def _inproj_kernel(x_ref, g_ref, w_ref, bd_ref, qg_ref, kg_ref, lb_ref, cos_ref, sa_ref, sb_ref,
                   qm_ref, k_ref, kb_ref, v_ref, vt_ref, iqm_ref, ikw_ref, ik2_ref,
                   hq_ref, kf_ref, lf_ref, hi_ref, hg_ref, ga_ref, gb_ref, h_scr, *, kv_cols):
    tm = x_ref.shape[0]
    x = x_ref[...]
    ms = jnp.mean(x * x, axis=-1, keepdims=True)
    h_scr[...] = (x * lax.rsqrt(ms + EPS) * g_ref[...]).astype(BF16)

    def chunk(jj):
        def run(fn):
            fn(jnp.dot(h_scr[...], w_ref[jj], preferred_element_type=F32))
        return run

    lane = lax.broadcasted_iota(I32, (tm, LANES), 1)
    lo_half = lane < HEAD_DIM_A

    def head_norm(t, gain):
        ms = jnp.dot((t * t).astype(BF16), bd_ref[...], preferred_element_type=F32)
        return t * lax.rsqrt(ms + EPS) * gain

    @chunk(_J_Q)
    def _(y):
        yn = head_norm(y, qg_ref[...])
        c, sa, sb = cos_ref[...], sa_ref[...], sb_ref[...]
        for p in range(WIDTH_A // LANES):
            r = _rot(yn[:, p * LANES:(p + 1) * LANES], c, sa, sb) * (HEAD_DIM_A ** -0.5 * LOG2E)
            qm_ref[:, (2 * p) * LANES:(2 * p + 1) * LANES] = jnp.where(lo_half, r, 0.0).astype(BF16)
            qm_ref[:, (2 * p + 1) * LANES:(2 * p + 2) * LANES] = jnp.where(lo_half, 0.0, r).astype(BF16)

    @chunk(_J_K)
    def _(y):
        yn = head_norm(y, kg_ref[...])
        c, sa, sb = cos_ref[...], sa_ref[...], sb_ref[...]
        for p in range(WIDTH_A // LANES):
            sl = slice(p * LANES, (p + 1) * LANES)
            r = _rot(yn[:, sl], c, sa, sb)
            if kv_cols:
                k_ref[0, sl, :] = r.T
            else:
                k_ref[:, sl] = r
            kb_ref[:, sl] = r.astype(BF16)

    @chunk(_J_V)
    def _(y):
        yt = y.T
        if kv_cols:
            v_ref[0] = yt
        else:
            v_ref[...] = y
        for u in range(vt_ref.shape[0]):
            vt_ref[u] = yt[:, u * vt_ref.shape[2]:(u + 1) * vt_ref.shape[2]].astype(BF16)

    @chunk(_J_IDX)
    def _(y):
        c, sa, sb = cos_ref[...], sa_ref[...], sb_ref[...]
        for p in range(2):
            r = _rot(y[:, p * LANES:(p + 1) * LANES], c, sa, sb)
            iqm_ref[:, (2 * p) * LANES:(2 * p + 1) * LANES] = jnp.where(lo_half, r, 0.0).astype(BF16)
            iqm_ref[:, (2 * p + 1) * LANES:(2 * p + 2) * LANES] = jnp.where(lo_half, 0.0, r).astype(BF16)
        r = _rot(y[:, 2 * LANES:3 * LANES], jnp.where(lo_half, c, 1.0),
                 jnp.where(lo_half, sa, 0.0), jnp.where(lo_half, sb, 0.0))
        if kv_cols:
            ikw_ref[0] = r.T
        else:
            ikw_ref[...] = r
        ik2_ref[...] = jnp.where(lo_half, r, pltpu.roll(r, HEAD_DIM_A, 1)).astype(BF16)

    @chunk(_J_HQ)
    def _(y):
        hq_ref[...] = (y * jax.nn.sigmoid(y)).astype(BF16)

    @chunk(_J_HF)
    def _(y):
        lb = lb_ref[...]
        lf_ref[...] = jnp.log(lb + (1.0 - lb) * jax.nn.sigmoid(y))
        kf_ref[...] = ((1.0 - lb) * jax.nn.sigmoid(-y)).astype(BF16)

    @chunk(_J_HI)
    def _(y):
        hi_ref[...] = y.astype(BF16)

    @chunk(_J_HG)
    def _(y):
        hg_ref[...] = (y * jax.nn.sigmoid(y)).astype(BF16)

    for jj, ref, half in ((_J_GA0, ga_ref, 0), (_J_GA1, ga_ref, 1), (_J_GB0, gb_ref, 0), (_J_GB1, gb_ref, 1)):
        @chunk(jj)
        def _(y, ref=ref, half=half):
            ref[:, half * IN_CHUNK:(half + 1) * IN_CHUNK] = jax.nn.sigmoid(y).astype(BF16)


def _inproj(x, gain, w12, bd, qg, kg, lb, cos_t, sa_t, sb_t, tm, kv_cols):
    n, d = x.shape
    nt = cos_t.shape[0] // tm
    row = lambda w: pl.BlockSpec((tm, w), lambda i: (i, 0))
    tab = pl.BlockSpec((tm, LANES), lambda i: (i % nt, 0))
    vt_tile = min(ATTN_TILE, tm)

    def rows(w, dt):
        return row(w), jax.ShapeDtypeStruct((n, w), dt)

    def kv(w):
        if kv_cols:
            return (pl.BlockSpec((1, w, tm), lambda i: (i // nt, 0, i % nt)),
                    jax.ShapeDtypeStruct((n // (nt * tm), w, nt * tm), F32))
        return rows(w, F32)

    outs = [rows(2 * WIDTH_A, BF16), kv(WIDTH_A), rows(WIDTH_A, BF16), kv(WIDTH_A),
            (pl.BlockSpec((tm // vt_tile, WIDTH_A, vt_tile), lambda i: (i, 0, 0)),
             jax.ShapeDtypeStruct((n // vt_tile, WIDTH_A, vt_tile), BF16)),
            rows(2 * N_HEADS_IDX * IDX_DIM, BF16), kv(LANES), rows(LANES, BF16),
            rows(KEYW_B, BF16), rows(KEYW_B, BF16), rows(KEYW_B, F32), rows(WIDTH_B, BF16), rows(WIDTH_B, BF16),
            rows(D_MODEL, BF16), rows(D_MODEL, BF16)]
    return pl.pallas_call(
        functools.partial(_inproj_kernel, kv_cols=kv_cols),
        grid=(n // tm,),
        in_specs=[row(d), _resident((1, d)), _resident(w12.shape), _resident(bd.shape),
                  _resident(qg.shape), _resident(kg.shape), _resident(lb.shape), tab, tab, tab],
        out_specs=[o[0] for o in outs],
        out_shape=[o[1] for o in outs],
        scratch_shapes=[pltpu.VMEM((tm, d), BF16)],
        compiler_params=_params("parallel"),
        name="inproj",
    )(x, gain.reshape(1, d), w12, bd, qg, kg, lb, cos_t, sa_t, sb_t)


def _tree(parts, op):
    while len(parts) > 1:
        parts = [op(a, b) for a, b in zip(parts[0::2], parts[1::2])] + ([parts[-1]] if len(parts) % 2 else [])
    return parts[0]


def _tree_sum(parts):
    return _tree(parts, jnp.add)


def _refine_threshold(cand, n_sel, count_fn, next_above, next_below):
    neg_flt_max_key = np.int32(-2139095040)
    key = jnp.maximum(cand, neg_flt_max_key)
    t = pltpu.bitcast(key ^ ((key >> 31) & np.int32(0x7FFFFFFF)), F32)

    def state(t):
        cge, cgt = count_fn(t, False), count_fn(t, True)
        lower = jnp.where(cand == INT_MIN, 0.0, -1.0)
        return cge, cgt, jnp.where(cgt >= n_sel, 1.0, jnp.where(cge < n_sel, lower, 0.0))

    def body(st):
        t, _, _, move = st
        t = jnp.where(move > 0.0, next_above(t), jnp.where(move < 0.0, next_below(t), t))
        return (t,) + state(t)

    t, cge, cgt, _ = lax.while_loop(lambda st: jnp.max(jnp.abs(st[3])) > 0.0, body, (t,) + state(t))
    return t, cge, cgt


def _search_threshold(count_ge, shape, n_sel):
    c0 = count_ge(jnp.zeros(shape, I32))
    ok = c0 >= n_sel
    cand = jnp.where(ok, np.int32(0), INT_MIN)
    ccnt = jnp.where(ok, c0, 0.0)

    def bit_body(i, carry):
        cand, ccnt = carry
        t = cand + (jnp.int32(1) << (30 - i))
        c = count_ge(t)
        ok = c >= n_sel
        return jnp.where(ok, t, cand), jnp.where(ok, c, ccnt)

    return lax.fori_loop(0, 31, bit_body, (cand, ccnt))


def _pattn_kernel(iqm_ref, ikw_ref, ik2_ref, qm_ref, k_ref, vt_ref, o_ref,
                  keys_scr, sc_scr, m_scr, a_scr, acc_scr, s_scr, p_scr, *, n_sel, tq):
    qi = pl.program_id(1)
    nkb = qi + 1
    krow = lax.broadcasted_iota(I32, (tq, tq), 0)
    qcol = lax.broadcasted_iota(I32, (tq, tq), 1)
    half = HEAD_DIM_A

    ikw_t = ikw_ref[0]
    wrow = [ikw_t[IDX_DIM + h:IDX_DIM + h + 1, :] * IDX_W_SCALE for h in range(N_HEADS_IDX)]

    def score_blk(kb, c):
        ik2 = ik2_ref[0, pl.ds(pl.multiple_of(kb * tq, tq), tq), :]
        sc = jnp.zeros((tq, tq), F32)
        for h in range(N_HEADS_IDX):
            d = lax.dot_general(ik2, iqm_ref[0, :, h * LANES:(h + 1) * LANES], NT_DIMS,
                                preferred_element_type=F32)
            sc = sc + jnp.maximum(d, 0.0) * wrow[h]
        off = jnp.where(kb < qi, jnp.int32(tq), jnp.int32(0))
        adm = krow <= qcol + off
        keys_scr[kb] = jnp.where(adm, _float_key(sc), INT_MIN)
        sc_scr[kb] = jnp.where(adm, sc, -jnp.inf)
        return c

    lax.fori_loop(0, nkb, score_blk, 0)
    keys_scr[nkb] = jnp.full((tq, tq), INT_MIN, I32)
    sc_scr[nkb] = jnp.full((tq, tq), -jnp.inf, F32)

    def over_rows(tile_fn, op, init):
        def block(kb):
            return _tree([tile_fn(kb, slice(r * SUBLANES, (r + 1) * SUBLANES))
                          for r in range(tq // SUBLANES)], op)

        def body(pi, acc):
            return op(acc, op(block(2 * pi), block(2 * pi + 1)))
        return lax.fori_loop(0, (nkb + 1) // 2, body, jnp.full((SUBLANES, tq), init, F32))

    def count_ge(t):
        acc = over_rows(lambda kb, rs: jnp.where(keys_scr[kb, rs, :] >= t, 1.0, 0.0), jnp.add, 0.0)
        return jnp.sum(acc, axis=0, keepdims=True)

    cand, _ = _search_threshold(count_ge, (1, tq), n_sel)

    def fcount(t, strict):
        hit = (lambda s: s > t) if strict else (lambda s: s >= t)
        acc = over_rows(lambda kb, rs: jnp.where(hit(sc_scr[kb, rs, :]), 1.0, 0.0), jnp.add, 0.0)
        return jnp.sum(acc, axis=0, keepdims=True)

    def next_above(t):
        def tile(kb, rs):
            s = sc_scr[kb, rs, :]
            return jnp.where(s > t, s, jnp.inf)
        return jnp.min(over_rows(tile, jnp.minimum, jnp.inf), axis=0, keepdims=True)

    def next_below(t):
        def tile(kb, rs):
            s = sc_scr[kb, rs, :]
            return jnp.where(s < t, s, -jnp.inf)
        return jnp.max(over_rows(tile, jnp.maximum, -jnp.inf), axis=0, keepdims=True)

    thr, cge, cgt = _refine_threshold(cand, n_sel, fcount, next_above, next_below)

    @pl.when(jnp.max(cge) > n_sel)
    def _():
        need = n_sel - cgt
        lower = jnp.where(qcol < krow, 1.0, 0.0).astype(BF16)

        def tie_blk(kb, seen):
            s = sc_scr[kb]
            tf = jnp.where(s == thr, 1.0, 0.0)
            rank = jnp.dot(lower, tf.astype(BF16), preferred_element_type=F32) + seen
            drop = jnp.where(rank >= need, tf, 0.0)
            sc_scr[kb] = jnp.where(drop > 0.0, -jnp.inf, s)
            return seen + jnp.sum(tf, axis=0, keepdims=True)

        lax.fori_loop(0, nkb, tie_blk, jnp.zeros((1, tq), F32))

    m_scr[...] = jnp.full(m_scr.shape, NEG_BIG, F32)
    acc_scr[...] = jnp.zeros(acc_scr.shape, F32)
    ones_half = jnp.ones((half, tq), BF16)

    def attn_blk(kb, c):
        bias = jnp.where(sc_scr[kb] >= thr, 0.0, NEG_BIG)
        ks = pl.ds(pl.multiple_of(kb * tq, tq), tq)
        for h in range(N_HEADS_A):
            pr = h // 2
            s_scr[h] = lax.dot_general(k_ref[0, ks, pr * LANES:(pr + 1) * LANES],
                                       qm_ref[0, :, h * LANES:(h + 1) * LANES], NT_DIMS,
                                       preferred_element_type=F32) + bias
        for h in range(N_HEADS_A):
            s = s_scr[h]
            m_old = m_scr[h]
            m_new = jnp.maximum(m_old, jnp.max(s, axis=0, keepdims=True))
            a_scr[h] = jnp.exp2(m_old - m_new)
            p_scr[h] = jnp.exp2(s - m_new).astype(BF16)
            m_scr[h] = m_new
        for h in range(N_HEADS_A):
            pr = h // 2
            vt = vt_ref[0, kb, pr * LANES:(pr + 1) * LANES, :]
            lhs = (jnp.concatenate([vt[:half], ones_half], axis=0) if h % 2 == 0
                   else jnp.concatenate([ones_half, vt[half:]], axis=0))
            acc_scr[h] = a_scr[h] * acc_scr[h] + jnp.dot(lhs, p_scr[h], preferred_element_type=F32)
        return c

    lax.fori_loop(0, nkb, attn_blk, 0)
    for pr in range(N_HEADS_A // 2):
        ae, ao = acc_scr[2 * pr], acc_scr[2 * pr + 1]
        ot = jnp.concatenate([ae[:half] / ae[half:], ao[half:] / ao[:half]], axis=0)
        o_ref[0, :, pr * LANES:(pr + 1) * LANES] = ot.T


def _prompt_attention(iqm, ikw, ik2, qm, kb, vt, n_sel, tq):
    b, t, _ = qm.shape
    nq = t // tq
    qblk = lambda w: pl.BlockSpec((1, tq, w), lambda bi, qi: (bi, qi, 0))
    full = lambda w: pl.BlockSpec((1, t, w), lambda bi, qi: (bi, 0, 0))
    return pl.pallas_call(
        functools.partial(_pattn_kernel, n_sel=n_sel, tq=tq),
        grid=(b, nq),
        in_specs=[qblk(iqm.shape[2]), pl.BlockSpec((1, LANES, tq), lambda bi, qi: (bi, 0, qi)), full(LANES),
                  qblk(qm.shape[2]), full(WIDTH_A),
                  pl.BlockSpec((1, nq, WIDTH_A, tq), lambda bi, qi: (bi, 0, 0, 0))],
        out_specs=qblk(WIDTH_A),
        out_shape=jax.ShapeDtypeStruct((b, t, WIDTH_A), F32),
        scratch_shapes=[
            pltpu.VMEM((nq + 1, tq, tq), I32),
            pltpu.VMEM((nq + 1, tq, tq), F32),
            pltpu.VMEM((N_HEADS_A, 1, tq), F32),
            pltpu.VMEM((N_HEADS_A, 1, tq), F32),
            pltpu.VMEM((N_HEADS_A, LANES, tq), F32),
            pltpu.VMEM((N_HEADS_A, tq, tq), F32),
            pltpu.VMEM((N_HEADS_A, tq, tq), BF16),
        ],
        compiler_params=_params("parallel", "arbitrary"),
        name="pattn",
    )(iqm, ikw, ik2, qm, kb, vt)


def _hgrn_kernel(q_ref, k_ref, g_ref, v_ref, s0_ref, o_ref, sn_ref, s_scr, *, tc):
    ti = pl.program_id(1)
    c_ = CHUNK_B

    streams = [(bi, h) for bi in range(q_ref.shape[0]) for h in range(N_HEADS_B)]

    @pl.when(ti == 0)
    def _():
        for si, (bi, h) in enumerate(streams):
            s_scr[si] = s0_ref[bi, h].T

    rowi = lax.broadcasted_iota(I32, (c_, LANES), 0)
    lanei = lax.broadcasted_iota(I32, (c_, LANES), 1)

    def chunk(ci, carry):
        rs = pl.ds(pl.multiple_of(ci * c_, c_), c_)
        for si, (bi, h) in enumerate(streams):
            hs = slice(h * LANES, (h + 1) * LANES)
            q = q_ref[bi, rs, hs].astype(F32)
            k = k_ref[bi, rs, hs].astype(F32)
            v = v_ref[bi, rs, hs]
            b = g_ref[bi, rs, hs] * LOG2E
            for sh in (1, 2, 4, 8):
                b = b + jnp.where(rowi >= sh, pltpu.roll(b, sh, 0), 0.0)
            b_last = b[c_ - 1:c_, :]
            st_old = s_scr[si]
            o = lax.dot_general((q * jnp.exp2(b)).astype(BF16), st_old.astype(BF16), NT_DIMS,
                                preferred_element_type=F32)
            cols = []
            for t in range(c_):
                e = jnp.exp2(jnp.minimum(b[t:t + 1, :] - b, 0.0))
                col = jnp.sum(e * k * q[t:t + 1, :], axis=1, keepdims=True)
                cols.append(jnp.where(lanei == t, col, 0.0))
            att = jnp.where(rowi <= lanei, _tree_sum(cols), 0.0).T[:c_]
            o_ref[bi, rs, hs] = o + jnp.dot(att.astype(BF16), v, preferred_element_type=F32)
            kd = k * jnp.exp2(b_last - b)
            upd = jnp.dot(v.astype(F32).T.astype(BF16), kd.astype(BF16), preferred_element_type=F32)
            s_scr[si] = st_old * jnp.exp2(b_last) + upd
        return carry

    lax.fori_loop(0, tc // c_, chunk, 0)

    @pl.when(ti == pl.num_programs(1) - 1)
    def _():
        for si, (bi, h) in enumerate(streams):
            sn_ref[bi, h] = s_scr[si].T


HGRN_BATCH_GROUP = 4


def _hgrn(q, k, g, v, s0, tc):
    b, t, w = q.shape
    bg = min(HGRN_BATCH_GROUP, b)
    blk = pl.BlockSpec((bg, tc, w), lambda bi, ti: (bi, ti, 0))
    st = pl.BlockSpec((bg, N_HEADS_B, KEY_DIM_B, VAL_DIM_B), lambda bi, ti: (bi, 0, 0, 0))
    return pl.pallas_call(
        functools.partial(_hgrn_kernel, tc=tc),
        grid=(b // bg, t // tc),
        in_specs=[blk, blk, blk, blk, st],
        out_specs=[blk, st],
        out_shape=[jax.ShapeDtypeStruct((b, t, w), F32), jax.ShapeDtypeStruct(s0.shape, F32)],
        scratch_shapes=[pltpu.VMEM((bg * N_HEADS_B, KEY_DIM_B, VAL_DIM_B), F32)],
        compiler_params=_params("parallel", "arbitrary"),
        name="hgrn",
    )(q, k, g, v, s0)


def _merge_kernel(attn_ref, o_ref, hg_ref, ga_ref, gb_ref, x_ref, on_ref, wpa_ref, wph_ref, wo_ref, y_ref):
    o = o_ref[...]
    parts = []
    for h in range(N_HEADS_B):
        oh = o[:, h * LANES:(h + 1) * LANES]
        ms = jnp.mean(oh * oh, axis=-1, keepdims=True)
        parts.append(oh * lax.rsqrt(ms + EPS))
    on = jnp.concatenate(parts, axis=1) * on_ref[...] * hg_ref[...]
    pa = jnp.dot(attn_ref[...].astype(BF16), wpa_ref[...], preferred_element_type=F32)
    ph = jnp.dot(on.astype(BF16), wph_ref[...], preferred_element_type=F32)
    merged = ga_ref[...] * pa + gb_ref[...] * ph
    y_ref[...] = x_ref[...] + jnp.dot(merged.astype(BF16), wo_ref[...], preferred_element_type=F32)


def _merge(attn, o, hg, ga, gb, x, onorm, wpa, wph, wo, tm):
    n, d = x.shape
    row = lambda w: pl.BlockSpec((tm, w), lambda i: (i, 0))
    const = lambda r, w: pl.BlockSpec((r, w), lambda i: (0, 0))
    return pl.pallas_call(
        _merge_kernel,
        grid=(n // tm,),
        in_specs=[row(WIDTH_A), row(WIDTH_B), row(WIDTH_B), row(d), row(d), row(d),
                  const(1, WIDTH_B), const(WIDTH_A, d), const(WIDTH_B, d), const(d, d)],
        out_specs=row(d),
        out_shape=jax.ShapeDtypeStruct((n, d), F32),
        compiler_params=_params("parallel"),
        name="merge",
    )(attn, o, hg, ga, gb, x, onorm, wpa, wph, wo)


def _rot_tables(pos):
    r = pos.shape[0]
    inv = ROPE_THETA ** (-jnp.arange(0, ROT_DIM, 2, dtype=F32) / ROT_DIM)
    ang = pos.astype(F32)[:, None] * inv[None, :]
    cos, sin = jnp.cos(ang), jnp.sin(ang)
    half = ROT_DIM // 2
    rest = HEAD_DIM_A - ROT_DIM
    one, z8, zr = jnp.ones((r, rest), F32), jnp.zeros((r, half), F32), jnp.zeros((r, rest), F32)
    c64 = jnp.concatenate([cos, cos, one], axis=1)
    a64 = jnp.concatenate([-sin, z8, zr], axis=1)
    b64 = jnp.concatenate([z8, sin, zr], axis=1)
    return tuple(jnp.concatenate([t, t], axis=1) for t in (c64, a64, b64))


def _prep_weights(ffn1_norm, ffn1_w_gate, ffn1_w_up, ffn1_w_down, mix_norm, w_in, q_norm, k_norm,
                  hgrn_lb, hgrn_o_norm, w_proj_attn, w_proj_hgrn, w_out,
                  ffn2_norm, ffn2_w_gate, ffn2_w_up, ffn2_w_down):
    l = 0
    q, k, v, iq, ik, iw, hq, hf, hi, hg, ga, gb = jnp.split(w_in[l], IN_OFFSETS, axis=1)
    pad = jnp.zeros((D_MODEL, IN_CHUNK - iq.shape[1] - ik.shape[1] - iw.shape[1]), F32)
    idx = jnp.concatenate([iq, ik, iw, pad], axis=1)
    w12 = jnp.stack([q, k, v, idx, hq, hf, hi, hg,
                     ga[:, :IN_CHUNK], ga[:, IN_CHUNK:], gb[:, :IN_CHUNK], gb[:, IN_CHUNK:]]).astype(BF16)
    head = np.arange(WIDTH_A) // HEAD_DIM_A
    bd = jnp.asarray((head[:, None] == head[None, :]).astype(np.float32) / HEAD_DIM_A, BF16)
    lb = jnp.cumsum(jax.nn.softmax(hgrn_lb.astype(F32), axis=0), axis=0)[l].reshape(1, KEYW_B)
    cols = rows = lambda a: a.astype(BF16)
    return dict(
        ffn1=(ffn1_norm[l], cols(ffn1_w_gate[l]), cols(ffn1_w_up[l]), rows(ffn1_w_down[l])),
        ffn2=(ffn2_norm[l], cols(ffn2_w_gate[l]), cols(ffn2_w_up[l]), rows(ffn2_w_down[l])),
        mix_norm=mix_norm[l], w12=w12, bd=bd,
        qg=jnp.tile(q_norm[l], N_HEADS_A).reshape(1, WIDTH_A),
        kg=jnp.tile(k_norm[l], N_HEADS_A).reshape(1, WIDTH_A),
        lb=lb,
        onorm=jnp.tile(hgrn_o_norm[l], N_HEADS_B).reshape(1, WIDTH_B),
        wpa=w_proj_attn[l].astype(BF16), wph=w_proj_hgrn[l].astype(BF16), wo=w_out[l].astype(BF16),
    )


def _pre(x, w, tabs, tm_ffn, tm_in, kv_cols):
    x1 = _ffn(x, *w["ffn1"], tm_ffn)
    names = ("qm", "k", "kb", "v", "vt", "iqm", "ikw", "ik2", "hq", "kf", "lf", "hi", "hg", "ga", "gb")
    outs = _inproj(x1, w["mix_norm"], w["w12"], w["bd"], w["qg"], w["kg"], w["lb"], *tabs, tm_in, kv_cols)
    return x1, dict(zip(names, outs))


def _post(x1, attn, o, p, w, tm_merge, tm_ffn):
    x2 = _merge(attn, o, p["hg"], p["ga"], p["gb"], x1, w["onorm"], w["wpa"], w["wph"], w["wo"], tm_merge)
    return _ffn(x2, *w["ffn2"], tm_ffn)


def _prompt_layer(x_prompt, w):
    b, t, d = x_prompt.shape
    n = b * t
    tabs = _rot_tables(jnp.arange(t, dtype=I32))
    x1, p = _pre(x_prompt.reshape(n, d), w, tabs, min(1024, n), min(512, t), True)
    r3 = lambda a: a.reshape(b, t, a.shape[-1])
    n_sel = min(TOPK_MAX, t // 4)
    tq = min(ATTN_TILE, t)
    vt = p["vt"].reshape(b, t // tq, WIDTH_A, tq)
    attn = _prompt_attention(r3(p["iqm"]), p["ikw"], r3(p["ik2"]), r3(p["qm"]), r3(p["kb"]), vt, n_sel, tq)
    s0 = jnp.zeros((b, N_HEADS_B, KEY_DIM_B, VAL_DIM_B), F32)
    o, s_new = _hgrn(r3(p["hq"]), r3(p["kf"]), r3(p["lf"]), r3(p["hi"]), s0, min(256, t))
    y = _post(x1, attn.reshape(n, WIDTH_A), o.reshape(n, WIDTH_B), p, w, min(512, n), min(1024, n))
    heads = lambda a: a.reshape(b, N_HEADS_A, HEAD_DIM_A, t).transpose(0, 3, 1, 2)
    return (y.reshape(b, t, d), heads(p["k"]), heads(p["v"]),
            p["ikw"][:, :IDX_DIM, :].transpose(0, 2, 1), s_new)


KV_PAGES_PER_STEP = 16
IDX_PAGE_UNROLL = 32


def _sidx_kernel(pt_ref, iq2_ref, w2_ref, iknew_ref, idx_hbm, sc_ref, thr_ref, keys_scr, page_buf, sems,
                 *, n_pages, n_sel, unroll):
    bi = pl.program_id(0)
    slot = bi % 2
    nblk = sc_ref.shape[1]

    def page_copy(b_, p, sl):
        return pltpu.make_async_copy(idx_hbm.at[pt_ref[b_, p]], page_buf.at[sl, p], sems.at[sl])

    def start_pages(b_, sl):
        def body(i, c):
            page_copy(b_, 2 * i, sl).start(priority=0)
            page_copy(b_, 2 * i + 1, sl).start(priority=1)
            return c
        lax.fori_loop(0, n_pages // 2, body, 0)

    @pl.when(bi == 0)
    def _():
        start_pages(0, 0)

    @pl.when(bi + 1 < pl.num_programs(0))
    def _():
        start_pages(bi + 1, 1 - slot)

    pltpu.make_async_copy(idx_hbm.at[pl.ds(0, n_pages)], page_buf.at[slot], sems.at[slot]).wait()
    iq2 = iq2_ref[0]
    w2 = w2_ref[0] * IDX_W_SCALE
    tq = iq2.shape[0] // N_HEADS_IDX

    def block_scores(ik_t):
        d = jnp.dot(iq2, ik_t, preferred_element_type=F32)
        r = jnp.maximum(d, 0.0) * w2
        sc = r[0:tq]
        for h in range(1, N_HEADS_IDX):
            sc = sc + r[h * tq:(h + 1) * tq]
        return sc

    def score_pages(g, c):
        for j in range(unroll):
            p = g * unroll + j
            sc = block_scores(page_buf[slot, p].astype(BF16))
            sc_ref[0, p] = sc
            keys_scr[p] = _float_key(sc)
        return c

    lax.fori_loop(0, n_pages // unroll, score_pages, 0)

    def finish():
        lane = lax.broadcasted_iota(I32, (tq, LANES), 1)
        row = lax.broadcasted_iota(I32, (tq, LANES), 0)
        sc = block_scores(iknew_ref[0])
        sc_ref[0, n_pages] = jnp.where(lane <= row, sc, -jnp.inf)
        keys_scr[n_pages] = jnp.where(lane <= row, _float_key(sc), INT_MIN)
        for u in range(n_pages + 1, nblk):
            sc_ref[0, u] = jnp.full((tq, LANES), -jnp.inf, F32)
            keys_scr[u] = jnp.full((tq, LANES), INT_MIN, I32)

        def over_blocks(tile_fn, op):
            accs = [tile_fn(u) for u in range(SUBLANES)]
            for u in range(SUBLANES, nblk):
                accs[u % SUBLANES] = op(accs[u % SUBLANES], tile_fn(u))
            return _tree(accs, op)

        def count_ge(t):
            return jnp.sum(over_blocks(lambda u: jnp.where(keys_scr[u] >= t, 1.0, 0.0), jnp.add),
                           axis=1, keepdims=True)

        def fcount(t, strict):
            hit = (lambda x: x > t) if strict else (lambda x: x >= t)
            return jnp.sum(over_blocks(lambda u: jnp.where(hit(sc_ref[0, u]), 1.0, 0.0), jnp.add),
                           axis=1, keepdims=True)

        def next_above(t):
            def tile(u):
                x = sc_ref[0, u]
                return jnp.where(x > t, x, jnp.inf)
            return jnp.min(over_blocks(tile, jnp.minimum), axis=1, keepdims=True)

        def next_below(t):
            def tile(u):
                x = sc_ref[0, u]
                return jnp.where(x < t, x, -jnp.inf)
            return jnp.max(over_blocks(tile, jnp.maximum), axis=1, keepdims=True)

        cand, _ = _search_threshold(count_ge, (tq, 1), n_sel)
        thr, cge, cgt = _refine_threshold(cand, n_sel, fcount, next_above, next_below)
        thr_ref[0] = jnp.broadcast_to(thr, (tq, LANES))

        @pl.when(jnp.max(cge) > n_sel)
        def _():
            need = n_sel - cgt
            r2 = lax.broadcasted_iota(I32, (LANES, LANES), 0)
            c2 = lax.broadcasted_iota(I32, (LANES, LANES), 1)
            upper = jnp.where(r2 < c2, 1.0, 0.0)

            def tie_blk(u, seen):
                x = sc_ref[0, u]
                tf = jnp.where(x == thr, 1.0, 0.0)
                rank = jnp.dot(tf, upper, preferred_element_type=F32) + seen
                drop = jnp.where(rank >= need, tf, 0.0)
                sc_ref[0, u] = jnp.where(drop > 0.0, -jnp.inf, x)
                return seen + jnp.sum(tf, axis=1, keepdims=True)

            lax.fori_loop(0, n_pages + 1, tie_blk, jnp.zeros((tq, 1), F32))

    finish()


def _sample_index(page_table, iq2, w2, iknew_t, idx_pages_t, n_sel):
    b, n_pages = page_table.shape
    tq = iq2.shape[1] // N_HEADS_IDX
    nblk = -(-(n_pages + 1) // SUBLANES) * SUBLANES
    per_b = lambda r, w: pl.BlockSpec((1, r, w), lambda bi, pt: (bi, 0, 0))
    return pl.pallas_call(
        functools.partial(_sidx_kernel, n_pages=n_pages, n_sel=n_sel, unroll=min(IDX_PAGE_UNROLL, n_pages)),
        grid_spec=pltpu.PrefetchScalarGridSpec(
            num_scalar_prefetch=1,
            grid=(b,),
            in_specs=[per_b(N_HEADS_IDX * tq, IDX_DIM), per_b(N_HEADS_IDX * tq, LANES), per_b(IDX_DIM, PAGE_SIZE),
                      pl.BlockSpec(memory_space=pl.ANY)],
            out_specs=[pl.BlockSpec((1, nblk, tq, LANES), lambda bi, pt: (bi, 0, 0, 0)),
                       pl.BlockSpec((1, tq, LANES), lambda bi, pt: (bi, 0, 0))],
            scratch_shapes=[pltpu.VMEM((nblk, tq, LANES), I32),
                            pltpu.VMEM((2, n_pages, IDX_DIM, PAGE_SIZE), F32),
                            pltpu.SemaphoreType.DMA((2,))],
        ),
        out_shape=[jax.ShapeDtypeStruct((b, nblk, tq, LANES), F32), jax.ShapeDtypeStruct((b, tq, LANES), F32)],
        compiler_params=_params("arbitrary"),
        name="sidx",
    )(page_table, iq2, w2, iknew_t, idx_pages_t)


def _sattn_kernel(pt_ref, qbd_ref, keys_ref, thr_ref, knew_ref, vnew_ref, *rest, ppb, n_pages):
    del pt_ref
    kpages, vpages = rest[:ppb], rest[ppb:2 * ppb]
    o_ref, m_scr, l_scr, acc_scr = rest[2 * ppb:]
    s = pl.program_id(1)
    nh = N_HEADS_A
    qbd = qbd_ref[0]
    tq = qbd.shape[0] // nh
    thr = jnp.concatenate([thr_ref[0, :, :1]] * nh, axis=0)

    @pl.when(s == 0)
    def _():
        m_scr[...] = jnp.full(m_scr.shape, NEG_BIG, F32)
        l_scr[...] = jnp.zeros(l_scr.shape, F32)
        acc_scr[...] = jnp.zeros(acc_scr.shape, F32)

    def pages_update(kts, vts, blks):
        kt = jnp.concatenate([a.astype(BF16) for a in kts], axis=1)
        vt = jnp.concatenate([a.astype(BF16) for a in vts], axis=1)
        kk = jnp.concatenate([keys_ref[0, blk] for blk in blks], axis=1)
        sc = jnp.dot(qbd, kt, preferred_element_type=F32)
        sc = jnp.where(jnp.concatenate([kk] * nh, axis=0) >= thr, sc, NEG_BIG)
        m_old = m_scr[:, :1]
        m_new = jnp.maximum(m_old, jnp.max(sc, axis=1, keepdims=True))
        alpha = jnp.exp2(m_old - m_new)
        p = jnp.exp2(sc - m_new)
        l_scr[...] = alpha * l_scr[...] + jnp.sum(p, axis=1, keepdims=True)
        pv = lax.dot_general(p.astype(BF16), vt, NT_DIMS, preferred_element_type=F32)
        acc_scr[...] = alpha * acc_scr[...] + pv
        m_scr[...] = jnp.broadcast_to(m_new, m_scr.shape)

    pages_update([r[...] for r in kpages], [r[...] for r in vpages], [s * ppb + j for j in range(ppb)])

    @pl.when(s == pl.num_programs(1) - 1)
    def _():
        pages_update([knew_ref[0]], [vnew_ref[0]], [n_pages])
        acc = acc_scr[...]
        o = jnp.concatenate([acc[h * tq:(h + 1) * tq, h * HEAD_DIM_A:(h + 1) * HEAD_DIM_A] for h in range(nh)],
                            axis=0)
        o_ref[0] = o / l_scr[:, :HEAD_DIM_A]


def _sample_attention(page_table, qbd, keys, thr, knew_t, vnew_t, k_pages_t, v_pages_t):
    b, n_pages = page_table.shape
    ppb = min(KV_PAGES_PER_STEP, n_pages)
    rows = qbd.shape[1]
    per_b3 = lambda r, w: pl.BlockSpec((1, r, w), lambda bi, s, pt: (bi, 0, 0))
    page = lambda j: pl.BlockSpec((None, WIDTH_A, PAGE_SIZE), lambda bi, s, pt, j=j: (pt[bi, s * ppb + j], 0, 0))
    return pl.pallas_call(
        functools.partial(_sattn_kernel, ppb=ppb, n_pages=n_pages),
        grid_spec=pltpu.PrefetchScalarGridSpec(
            num_scalar_prefetch=1,
            grid=(b, n_pages // ppb),
            in_specs=[per_b3(rows, WIDTH_A),
                      pl.BlockSpec((1,) + keys.shape[1:], lambda bi, s, pt: (bi, 0, 0, 0)),
                      per_b3(thr.shape[1], LANES), per_b3(WIDTH_A, PAGE_SIZE), per_b3(WIDTH_A, PAGE_SIZE)]
            + [page(j) for j in range(ppb)] * 2,
            out_specs=per_b3(rows, HEAD_DIM_A),
            scratch_shapes=[pltpu.VMEM((rows, LANES), F32), pltpu.VMEM((rows, LANES), F32),
                            pltpu.VMEM((rows, WIDTH_A), F32)],
        ),
        out_shape=jax.ShapeDtypeStruct((b, rows, HEAD_DIM_A), F32),
        compiler_params=_params("parallel", "arbitrary"),
        name="sattn",
    )(page_table, qbd, keys, thr, knew_t, vnew_t, *([k_pages_t] * ppb), *([v_pages_t] * ppb))


def _unmask_heads(xm, n_heads):
    x = xm.reshape(xm.shape[0], n_heads, LANES)
    return jnp.stack([x[:, h, (h % 2) * HEAD_DIM_A:(h % 2 + 1) * HEAD_DIM_A] for h in range(n_heads)], axis=1)


def _sample_layer(x_sample, cache_k, cache_v, cache_idx_k, state, page_table, w):
    b, t, d = x_sample.shape
    n = b * t
    n_pages = page_table.shape[1]
    past = n_pages * PAGE_SIZE
    pos = past + jnp.arange(t, dtype=I32)
    x1, p = _pre(x_sample.reshape(n, d), w, _rot_tables(jnp.tile(pos, b)), n, n, False)

    iq2 = _unmask_heads(p["iqm"], N_HEADS_IDX).reshape(b, t, N_HEADS_IDX, IDX_DIM)
    iq2 = iq2.transpose(0, 2, 1, 3).reshape(b, N_HEADS_IDX * t, IDX_DIM)
    w2 = p["ikw"][:, IDX_DIM:IDX_DIM + N_HEADS_IDX].reshape(b, t, N_HEADS_IDX).transpose(0, 2, 1)
    w2 = jnp.broadcast_to(w2.reshape(b, N_HEADS_IDX * t, 1), (b, N_HEADS_IDX * t, LANES))
    new_page = lambda a: jnp.pad(a.reshape(b, t, -1).transpose(0, 2, 1), ((0, 0), (0, 0), (0, PAGE_SIZE - t)))
    n_sel = min(TOPK_MAX, (past + t) // 4)
    keys, thr = _sample_index(page_table, iq2, w2, new_page(p["ik2"][:, :IDX_DIM]),
                              cache_idx_k[0].transpose(0, 2, 1), n_sel)

    q2 = _unmask_heads(p["qm"], N_HEADS_A).reshape(b, t, N_HEADS_A, HEAD_DIM_A).transpose(0, 2, 1, 3)
    eye = jnp.eye(N_HEADS_A, dtype=q2.dtype)
    qbd = (q2[:, :, :, None, :] * eye[None, :, None, :, None]).reshape(b, N_HEADS_A * t, WIDTH_A)
    pages_t = lambda c: c[0].transpose(0, 2, 3, 1).reshape(-1, WIDTH_A, PAGE_SIZE)
    o2 = _sample_attention(page_table, qbd, keys, thr, new_page(p["k"]), new_page(p["v"]),
                           pages_t(cache_k), pages_t(cache_v))
    attn = o2.reshape(b, N_HEADS_A, t, HEAD_DIM_A).transpose(0, 2, 1, 3).reshape(n, WIDTH_A)

    tp = -(-t // CHUNK_B) * CHUNK_B
    r3 = lambda a: jnp.pad(a.reshape(b, t, a.shape[-1]), ((0, 0), (0, tp - t), (0, 0)))
    o, s_new = _hgrn(r3(p["hq"]), r3(p["kf"]), r3(p["lf"]), r3(p["hi"]), state, tp)
    y = _post(x1, attn, o[:, :t].reshape(n, WIDTH_B), p, w, n, n)
    return (y.reshape(b, t, d),
            p["k"].reshape(b, t, N_HEADS_A, HEAD_DIM_A), p["v"].reshape(b, t, N_HEADS_A, HEAD_DIM_A),
            p["ikw"][:, :IDX_DIM].reshape(b, t, IDX_DIM), s_new)


def kernel(x_prompt, x_sample, cache_k, cache_v, cache_idx_k, state_hgrn, page_table,
           ffn1_norm, ffn1_w_gate, ffn1_w_up, ffn1_w_down, mix_norm, w_in, q_norm, k_norm,
           hgrn_lb, hgrn_o_norm, w_proj_attn, w_proj_hgrn, w_out,
           ffn2_norm, ffn2_w_gate, ffn2_w_up, ffn2_w_down):
    w = _prep_weights(ffn1_norm, ffn1_w_gate, ffn1_w_up, ffn1_w_down, mix_norm, w_in, q_norm, k_norm,
                      hgrn_lb, hgrn_o_norm, w_proj_attn, w_proj_hgrn, w_out,
                      ffn2_norm, ffn2_w_gate, ffn2_w_up, ffn2_w_down)
    yp, kp, vp, ikp, sp = _prompt_layer(x_prompt, w)
    ys, ks, vs, iks, ss = _sample_layer(x_sample, cache_k, cache_v, cache_idx_k, state_hgrn[0], page_table, w)
    return (yp, ys, kp[None], vp[None], ikp[None], sp[None], ks[None], vs[None], iks[None], ss[None])
```
